```python
import math
import jax, jax.numpy as jnp
from jax import lax
import numpy as np


D_MODEL = 1024
BATCH = 32
SEQ = 2048
DEPTH = 2

GRID_W = 64
CTX_LEN = 256
HEAD_DIM = 128
ROPE_THETA = 10000.0
EPS = 1e-6
N_MOD = 9
D_FF = 2816

A_HEADS = 4
A_KV_HEADS = 2
A_GROUP = A_HEADS // A_KV_HEADS
Q_BLOCK = 128
POOL_WINDOWS = (2, 4, 8, 16)
POOL_GROUP = 128
POOL_DIM = POOL_GROUP * len(POOL_WINDOWS)
A_Q_DIM = A_HEADS * HEAD_DIM
A_KV_DIM = A_KV_HEADS * HEAD_DIM
AB_IN = A_Q_DIM + 2 * A_KV_DIM + POOL_DIM
AB_OUT = A_Q_DIM + POOL_DIM

C_HEADS = 8
C_DIM = C_HEADS * HEAD_DIM
CONV_W = 3
CHUNK = 64
C_IN = 4 * C_DIM + 4 * C_HEADS

N_EVEN = (DEPTH + 1) // 2
N_ODD = DEPTH // 2

kernel_name = 'hybrid_dit_gqa_pool_gdn_macaron'


def rms_norm(t, g):
    tf = t.astype(jnp.float32)
    y = tf * lax.rsqrt(jnp.mean(tf * tf, axis=-1, keepdims=True) + EPS)
    return (y * g.astype(jnp.float32)).astype(t.dtype)


def l2_norm(t):
    return t * lax.rsqrt(jnp.sum(t * t, axis=-1, keepdims=True) + EPS)


def adaln(cond, w, b):
    m = jax.nn.silu(cond) @ w + b
    return m.reshape(cond.shape[0], 1, N_MOD, cond.shape[-1])


def modulate(t, g, m, s):
    return rms_norm(t, g) * (1 + m[:, :, 3 * s + 1]) + m[:, :, 3 * s]


def swiglu(t, wg, wu, wd):
    return (jax.nn.silu(t @ wg) * (t @ wu)) @ wd


def axial_rope_tables(rows):
    row = jnp.repeat(jnp.arange(rows), GRID_W).astype(jnp.float32)
    col = jnp.tile(jnp.arange(GRID_W), rows).astype(jnp.float32)
    half = HEAD_DIM // 2
    inv_freq = jnp.power(ROPE_THETA, -jnp.arange(0, half, 2, dtype=jnp.float32) / half)
    ang = jnp.stack([row[:, None] * inv_freq, col[:, None] * inv_freq], 0)
    return jnp.cos(ang), jnp.sin(ang)


def apply_axial_rope(t, cos, sin):
    tf = t.astype(jnp.float32)
    segs = tf.reshape(tf.shape[:-1] + (2, 2, HEAD_DIM // 4))
    x1, x2 = segs[..., 0, :], segs[..., 1, :]
    cb = cos.transpose(1, 0, 2)[None, :, None]
    sb = sin.transpose(1, 0, 2)[None, :, None]
    out = jnp.stack([x1 * cb - x2 * sb, x2 * cb + x1 * sb], axis=-2)
    return out.reshape(t.shape).astype(t.dtype)


def block_attention(q, k, v):
    B, N, Hq, Dh = q.shape
    nb = N // Q_BLOCK
    qb = q.reshape(B, nb, Q_BLOCK, A_KV_HEADS, A_GROUP, Dh).transpose(1, 0, 2, 3, 4, 5)
    scale = Dh ** -0.5

    def one_block(qblk):
        s = jnp.einsum('bqhgd,bkhd->bhgqk', qblk, k).astype(jnp.float32) * scale
        p = jax.nn.softmax(s, axis=-1).astype(v.dtype)
        return jnp.einsum('bhgqk,bkhd->bqhgd', p, v)

    o = lax.map(one_block, qb)
    return o.transpose(1, 0, 2, 3, 4, 5).reshape(B, N, Hq * Dh)


def multiscale_pool(u, pool_w, pool_scale):
    B, N, _ = u.shape
    uf = u.astype(jnp.float32).reshape(B, N, len(POOL_WINDOWS), POOL_GROUP)
    csum = jnp.concatenate([jnp.zeros_like(uf[:, :1]), jnp.cumsum(uf, axis=1)], axis=1)
    t = jnp.arange(N)
    means = []
    for gi, w in enumerate(POOL_WINDOWS):
        lo = jnp.clip(t - w // 2, 0, N)
        hi = jnp.clip(t + w - w // 2, 0, N)
        cg = csum[:, :, gi]
        cnt = (hi - lo).astype(jnp.float32)[None, :, None]
        means.append((cg[:, hi] - cg[:, lo]) / cnt)
    pooled = (jnp.stack(means, axis=2) - uf).astype(u.dtype)
    y = jnp.einsum('bngc,gcd->bngd', pooled, pool_w).reshape(B, N, POOL_DIM)
    return y * pool_scale


def mix_attn_pool(xn, hn, cos, sin, w_in, q_g, k_g, pool_w, pool_scale, w_out, ctx_out):
    def project(t):
        p = t @ w_in
        B, N, _ = p.shape
        q = p[..., :A_Q_DIM].reshape(B, N, A_HEADS, HEAD_DIM)
        k = p[..., A_Q_DIM:A_Q_DIM + A_KV_DIM].reshape(B, N, A_KV_HEADS, HEAD_DIM)
        v = p[..., A_Q_DIM + A_KV_DIM:A_Q_DIM + 2 * A_KV_DIM].reshape(B, N, A_KV_HEADS, HEAD_DIM)
        u = p[..., A_Q_DIM + 2 * A_KV_DIM:]
        return rms_norm(q, q_g), rms_norm(k, k_g), v, u

    qx, kx, vx, ux = project(xn)
    qh, kh, vh, uh = project(hn)
    qx = apply_axial_rope(qx, cos, sin)
    kx = apply_axial_rope(kx, cos, sin)
    k_all = jnp.concatenate([kh, kx], axis=1)
    v_all = jnp.concatenate([vh, vx], axis=1)
    yx = jnp.concatenate([block_attention(qx, k_all, v_all), multiscale_pool(ux, pool_w, pool_scale)], axis=-1) @ w_out
    yh = None
    if ctx_out:
        yh = jnp.concatenate([block_attention(qh, kh, vh), multiscale_pool(uh, pool_w, pool_scale)], axis=-1) @ w_out
    return yx, yh


def short_conv(t, w):
    return lax.conv_general_dilated(t, w[:, None, :], window_strides=(1,),
                                    padding=((CONV_W // 2, CONV_W // 2),),
                                    dimension_numbers=('NWC', 'WIO', 'NWC'),
                                    feature_group_count=t.shape[-1])


def gated_delta_chunked(q, k, v, g, beta, s0, need_out):
    B, N, H, Dk = q.shape
    Dv = v.shape[-1]
    nc = N // CHUNK

    def chunked(t):
        t = t.reshape((B, nc, CHUNK, H) + t.shape[3:])
        return jnp.moveaxis(t, 3, 2)

    q, k, v, g, beta = (chunked(t) for t in (q, k, v, g, beta))
    gc = jnp.cumsum(g, axis=-1)
    idx = jnp.arange(CHUNK)
    incl = idx[:, None] >= idx[None, :]
    strict = idx[:, None] > idx[None, :]
    decay = jnp.exp(jnp.where(incl, gc[..., :, None] - gc[..., None, :], -jnp.inf))
    kb = k * beta[..., None]
    a_mat = jnp.where(strict, jnp.einsum('bnhid,bnhjd->bnhij', kb, k) * decay, 0.0) + jnp.eye(CHUNK, dtype=jnp.float32)
    rhs = jnp.concatenate([v * beta[..., None], kb * jnp.exp(gc)[..., None]], axis=-1)
    sol = lax.linalg.triangular_solve(a_mat, rhs, left_side=True, lower=True, unit_diagonal=True)
    u, w = sol[..., :Dv], sol[..., Dv:]
    k_tail = k * jnp.exp(gc[..., -1:] - gc)[..., None]
    g_tot = jnp.exp(gc[..., -1])
    xs = [u, w, k_tail, g_tot]
    if need_out:
        qk = jnp.where(incl, jnp.einsum('bnhid,bnhjd->bnhij', q, k) * decay, 0.0)
        xs += [q * jnp.exp(gc)[..., None], qk]
    xs = tuple(jnp.moveaxis(t, 1, 0) for t in xs)

    def step(state, inp):
        u_c, w_c, kt_c, gt_c = inp[:4]
        v_new = u_c - jnp.einsum('bhcd,bhde->bhce', w_c, state)
        new_state = state * gt_c[..., None, None] + jnp.einsum('bhcd,bhce->bhde', kt_c, v_new)
        if need_out:
            qd_c, qk_c = inp[4:]
            o = jnp.einsum('bhcd,bhde->bhce', qd_c, state) + jnp.einsum('bhij,bhje->bhie', qk_c, v_new)
            return new_state, o
        return new_state, None

    s_fin, o = lax.scan(step, s0, xs)
    if need_out:
        o = jnp.moveaxis(jnp.moveaxis(o, 0, 1), 2, 3).reshape(B, N, H, Dv)
    return o, s_fin


def mix_gated_delta(xn, hn, w_in, conv_w, a_log, dt_bias, o_g, w_out, ctx_out):
    def prep(t):
        p = t @ w_in
        B, N, _ = p.shape
        qkv = jax.nn.silu(short_conv(p[..., :3 * C_DIM], conv_w)).astype(jnp.float32)
        qkv = qkv.reshape(B, N, 3, C_HEADS, HEAD_DIM)
        q = l2_norm(qkv[:, :, 0]) * (HEAD_DIM ** -0.5)
        k = l2_norm(qkv[:, :, 1])
        v = qkv[:, :, 2]
        z = p[..., 3 * C_DIM:4 * C_DIM].reshape(B, N, C_HEADS, HEAD_DIM)
        ab = p[..., 4 * C_DIM:].astype(jnp.float32).reshape(B, N, 2, 2, C_HEADS)
        g = -jnp.exp(a_log.astype(jnp.float32)) * jax.nn.softplus(ab[:, :, :, 0] + dt_bias.astype(jnp.float32))
        beta = jax.nn.sigmoid(ab[:, :, :, 1])
        return q, k, v, z, g, beta

    qx, kx, vx, zx, gx, bx = prep(xn)
    qh, kh, vh, zh, gh, bh = prep(hn)
    s0 = jnp.zeros((xn.shape[0], C_HEADS, HEAD_DIM, HEAD_DIM), jnp.float32)
    outs_x, outs_h = [], []
    for d in range(2):
        flip = (lambda t: jnp.flip(t, axis=1)) if d == 1 else (lambda t: t)
        o_h, s_h = gated_delta_chunked(flip(qh), flip(kh), flip(vh), flip(gh[:, :, d]), flip(bh[:, :, d]), s0, ctx_out)
        o_x, _ = gated_delta_chunked(flip(qx), flip(kx), flip(vx), flip(gx[:, :, d]), flip(bx[:, :, d]), s_h, True)
        outs_x.append(flip(o_x))
        if ctx_out:
            outs_h.append(flip(o_h))

    def finish(o, z, ref):
        y = rms_norm(o, o_g) * jax.nn.silu(z.astype(jnp.float32))
        return y.reshape(ref.shape[0], ref.shape[1], C_DIM).astype(ref.dtype) @ w_out

    yx = finish(outs_x[0] + outs_x[1], zx, xn)
    yh = finish(outs_h[0] + outs_h[1], zh, hn) if ctx_out else None
    return yx, yh


def setup_inputs(seed: int = 0) -> dict:
    key = jax.random.key(seed)
    ks = jax.random.split(key, 24)
    D, F = D_MODEL, D_FF

    def nrm(k, shape, s):
        return jax.random.normal(k, shape, jnp.float32) * s

    dt = jnp.exp(jax.random.uniform(ks[19], (N_ODD, 2, C_HEADS), jnp.float32,
                                    minval=math.log(1e-3), maxval=math.log(1e-1)))
    return {
        'x': nrm(ks[0], (BATCH, SEQ, D), 1.0),
        'c': nrm(ks[1], (BATCH, D), 1.0),
        'ctx': nrm(ks[2], (BATCH, CTX_LEN, D), 1.0),
        'c_ctx': nrm(ks[3], (D,), 1.0),
        'w_mod': nrm(ks[4], (DEPTH, D, N_MOD * D), 0.5 * D ** -0.5),
        'b_mod': nrm(ks[5], (DEPTH, N_MOD * D), 0.01),
        'norm_g': 1.0 + nrm(ks[6], (DEPTH, 3, D), 0.02),
        'ffn_wg': nrm(ks[7], (DEPTH, 2, D, F), D ** -0.5),
        'ffn_wu': nrm(ks[8], (DEPTH, 2, D, F), D ** -0.5),
        'ffn_wd': nrm(ks[9], (DEPTH, 2, F, D), F ** -0.5),
        'ab_w_in': nrm(ks[10], (N_EVEN, D, AB_IN), D ** -0.5),
        'ab_q_norm': 1.0 + nrm(ks[11], (N_EVEN, HEAD_DIM), 0.02),
        'ab_k_norm': 1.0 + nrm(ks[12], (N_EVEN, HEAD_DIM), 0.02),
        'pool_w': nrm(ks[13], (N_EVEN, len(POOL_WINDOWS), POOL_GROUP, POOL_GROUP), POOL_GROUP ** -0.5),
        'pool_scale': 1.0 + nrm(ks[14], (N_EVEN, POOL_DIM), 0.1),
        'ab_w_out': nrm(ks[15], (N_EVEN, AB_OUT, D), AB_OUT ** -0.5),
        'gdn_w_in': nrm(ks[16], (N_ODD, D, C_IN), D ** -0.5),
        'gdn_conv_w': nrm(ks[17], (N_ODD, CONV_W, 3 * C_DIM), CONV_W ** -0.5),
        'gdn_a_log': jnp.log(jax.random.uniform(ks[18], (N_ODD, 2, C_HEADS), jnp.float32, minval=1.0, maxval=16.0)),
        'gdn_dt_bias': dt + jnp.log(-jnp.expm1(-dt)),
        'gdn_o_norm': 1.0 + nrm(ks[20], (N_ODD, HEAD_DIM), 0.02),
        'gdn_w_out': nrm(ks[21], (N_ODD, C_DIM, D), C_DIM ** -0.5),
    }


def reference(x, c, ctx, c_ctx, w_mod, b_mod, norm_g, ffn_wg, ffn_wu, ffn_wd,
              ab_w_in, ab_q_norm, ab_k_norm, pool_w, pool_scale, ab_w_out,
              gdn_w_in, gdn_conv_w, gdn_a_log, gdn_dt_bias, gdn_o_norm, gdn_w_out):
    rows = x.shape[1] // GRID_W
    cos, sin = axial_rope_tables(rows)
    h = ctx
    for i in range(DEPTH):
        last = i == DEPTH - 1
        mx = adaln(c, w_mod[i], b_mod[i])
        mh = adaln(c_ctx[None], w_mod[i], b_mod[i])
        x = x + 0.5 * mx[:, :, 2] * swiglu(modulate(x, norm_g[i, 0], mx, 0), ffn_wg[i, 0], ffn_wu[i, 0], ffn_wd[i, 0])
        h = h + 0.5 * mh[:, :, 2] * swiglu(modulate(h, norm_g[i, 0], mh, 0), ffn_wg[i, 0], ffn_wu[i, 0], ffn_wd[i, 0])
        xn = modulate(x, norm_g[i, 1], mx, 1)
        hn = modulate(h, norm_g[i, 1], mh, 1)
        j = i // 2
        if i % 2 == 0:
            yx, yh = mix_attn_pool(xn, hn, cos, sin, ab_w_in[j], ab_q_norm[j], ab_k_norm[j],
                                   pool_w[j], pool_scale[j], ab_w_out[j], not last)
        else:
            yx, yh = mix_gated_delta(xn, hn, gdn_w_in[j], gdn_conv_w[j], gdn_a_log[j], gdn_dt_bias[j],
                                     gdn_o_norm[j], gdn_w_out[j], not last)
        x = x + mx[:, :, 5] * yx
        x = x + 0.5 * mx[:, :, 8] * swiglu(modulate(x, norm_g[i, 2], mx, 2), ffn_wg[i, 1], ffn_wu[i, 1], ffn_wd[i, 1])
        if not last:
            h = h + mh[:, :, 5] * yh
            h = h + 0.5 * mh[:, :, 8] * swiglu(modulate(h, norm_g[i, 2], mh, 2), ffn_wg[i, 1], ffn_wu[i, 1], ffn_wd[i, 1])
    return x
```

```python
import functools
import math

import jax
import jax.numpy as jnp
from jax import lax
from jax.experimental import pallas as pl
from jax.experimental.pallas import tpu as pltpu

F32 = jnp.float32
BF16 = jnp.bfloat16

EPS = 1e-6
HEAD_DIM = 128
GRID_W = 64
ROPE_THETA = 10000.0
N_MOD = 9
A_HEADS = 4
A_KV_HEADS = 2
POOL_WINDOWS = (2, 4, 8, 16)
POOL_GROUP = 128
C_HEADS = 8
CHUNK = 64

SUBLANES = 8
VMEM_LIMIT_BYTES = 56 * 1024 * 1024
HALO = SUBLANES

NEG_BIG = -1e30


def _params(n_grid, vmem=VMEM_LIMIT_BYTES):
    return pltpu.CompilerParams(dimension_semantics=("arbitrary",) * n_grid, vmem_limit_bytes=vmem)


def _resident(shape):
    nd = len(shape)
    return pl.BlockSpec(shape, lambda *_: (0,) * nd, pipeline_mode=pl.Buffered(1))


def _sigmoid(t):
    return 1.0 / (1.0 + jnp.exp(-t))


def _silu(t):
    return t * _sigmoid(t)


def _rms_mod(x, g, shift, scale):
    y = x * lax.rsqrt(jnp.mean(x * x, axis=-1, keepdims=True) + EPS)
    return (y * g) * (1.0 + scale) + shift


def _dot(a, b):
    return jnp.dot(a, b, preferred_element_type=F32)


def _dot_nt(a, b):
    return lax.dot_general(a, b, (((1,), (1,)), ((), ())), preferred_element_type=F32)


def _dot_tn(a, b):
    return lax.dot_general(a, b, (((0,), (0,)), ((), ())), preferred_element_type=F32)


def _mod_kernel(c_ref, w_ref, b_ref, o_ref):
    a = _silu(c_ref[...]).astype(BF16)
    o_ref[0] = _dot(a, w_ref[0].astype(BF16)) + b_ref[0]


def _modulation(cond, w_mod, b_mod):
    depth, d, nd = w_mod.shape
    rows = cond.shape[0]
    bn = d
    return pl.pallas_call(
        _mod_kernel,
        out_shape=jax.ShapeDtypeStruct((depth, rows, nd), F32),
        grid=(depth, nd // bn),
        in_specs=[
            pl.BlockSpec((rows, d), lambda i, j: (0, 0)),
            pl.BlockSpec((1, d, bn), lambda i, j: (i, 0, j)),
            pl.BlockSpec((1, 1, bn), lambda i, j: (i, 0, j)),
        ],
        out_specs=pl.BlockSpec((1, rows, bn), lambda i, j: (i, 0, j)),
        compiler_params=_params(2),
        name="adaln_mod",
    )(cond, w_mod, b_mod.reshape(depth, 1, nd))


def _ffn_kernel(*refs, s, n_mix):
    x_ref, mod_ref, g_ref, wg_ref, wu_ref, wd_ref = refs[:6]
    mix_refs = refs[6:6 + n_mix]
    wo_ref = refs[6 + n_mix] if n_mix else None
    o_ref = refs[-1]
    x = x_ref[0]
    mod = mod_ref[0]
    if n_mix:
        y = None
        k0 = 0
        for m_ref in mix_refs:
            kk = m_ref.shape[-1]
            part = _dot(m_ref[0], wo_ref[k0:k0 + kk, :])
            y = part if y is None else y + part
            k0 += kk
        x = x + mod[5:6] * y
    xn = _rms_mod(x, g_ref[...], mod[3 * s:3 * s + 1], mod[3 * s + 1:3 * s + 2]).astype(BF16)
    gate = _dot(xn, wg_ref[...])
    up = _dot(xn, wu_ref[...])
    hid = (_silu(gate) * up).astype(BF16)
    o_ref[0] = x + (0.5 * mod[3 * s + 2:3 * s + 3]) * _dot(hid, wd_ref[...])


def _ffn(x, mod, g, wg, wu, wd, *, s, tm, mix=(), w_out=None):
    b, n, d = x.shape
    f = wg.shape[1]
    mod_map = (lambda i, j: (i, 0, 0)) if mod.shape[0] == b else (lambda i, j: (0, 0, 0))
    in_specs = [
        pl.BlockSpec((1, tm, d), lambda i, j: (i, j, 0)),
        pl.BlockSpec((1, N_MOD, d), mod_map),
        _resident((1, d)),
        _resident((d, f)),
        _resident((d, f)),
        _resident((f, d)),
    ]
    args = [x, mod, g, wg, wu, wd]
    for m in mix:
        in_specs.append(pl.BlockSpec((1, tm, m.shape[-1]), lambda i, j: (i, j, 0)))
        args.append(m)
    if mix:
        in_specs.append(_resident(w_out.shape))
        args.append(w_out)
    return pl.pallas_call(
        functools.partial(_ffn_kernel, s=s, n_mix=len(mix)),
        out_shape=jax.ShapeDtypeStruct(x.shape, F32),
        grid=(b, n // tm),
        in_specs=in_specs,
        out_specs=pl.BlockSpec((1, tm, d), lambda i, j: (i, j, 0)),
        compiler_params=_params(2),
        name="swiglu_half_step",
    )(*args)


def _halo_specs(n, tm, d):
    per = tm // HALO
    last = n // HALO - 1
    return [
        pl.BlockSpec((1, tm, d), lambda i, j: (i, j, 0)),
        pl.BlockSpec((1, HALO, d), lambda i, j: (i, jnp.maximum(j * per - 1, 0), 0)),
        pl.BlockSpec((1, HALO, d), lambda i, j: (i, jnp.minimum((j + 1) * per, last), 0)),
    ]


def _normed_ext_tile(x_ref, xp_ref, xn_ref, mod, g):
    xe = jnp.concatenate([xp_ref[0], x_ref[0], xn_ref[0]], axis=0)
    return _rms_mod(xe, g, mod[3:4], mod[4:5]).astype(BF16)


def _zero_invalid_halo(pe, tm):
    j = pl.program_id(1)
    row = lax.broadcasted_iota(jnp.int32, (tm + 2 * HALO, 1), 0)
    dead = ((row < HALO) & (j == 0)) | ((row >= tm + HALO) & (j == pl.num_programs(1) - 1))
    return jnp.where(dead, 0.0, pe)


def _swap32(t):
    lane = lax.broadcasted_iota(jnp.int32, t.shape, 1)
    return jnp.where((lane % 64) < 32, pltpu.roll(t, HEAD_DIM - 32, 1), pltpu.roll(t, 32, 1))


def _ab_proj_kernel(*refs, tm, n_seq, rope):
    x_ref, xp_ref, xn_ref, mod_ref, g_ref, w_ref, qg_ref, kg_ref, pw_ref, ps_ref = refs[:10]
    if rope:
        cos_ref, sin_ref = refs[10:12]
    q_ref, k_ref, v_ref, y_ref, ue_ref = refs[-5:]
    mod = mod_ref[0]
    xe = _normed_ext_tile(x_ref, xp_ref, xn_ref, mod, g_ref[...])
    p = _dot(xe, w_ref[...])
    pm = p[HALO:HALO + tm]
    nq = A_HEADS * HEAD_DIM
    nkv = A_KV_HEADS * HEAD_DIM

    def head_norm(t, gain):
        t = t * lax.rsqrt(jnp.mean(t * t, axis=-1, keepdims=True) + EPS) * gain
        if rope:
            t = t * cos_ref[...] + _swap32(t) * sin_ref[...]
        return t.astype(BF16)

    for h in range(A_HEADS):
        sl = slice(h * HEAD_DIM, (h + 1) * HEAD_DIM)
        q_ref[0, :, sl] = head_norm(pm[:, sl], qg_ref[...])
    for h in range(A_KV_HEADS):
        sl = slice(h * HEAD_DIM, (h + 1) * HEAD_DIM)
        k_ref[0, :, sl] = head_norm(pm[:, nq + h * HEAD_DIM:nq + (h + 1) * HEAD_DIM], kg_ref[...])
    v_ref[0] = pm[:, nq + nkv:nq + 2 * nkv].astype(BF16)

    u0 = nq + 2 * nkv
    ue_ref[...] = _zero_invalid_halo(p[:, u0:], tm)
    pos = pl.program_id(1) * tm + lax.broadcasted_iota(jnp.int32, (tm, 1), 0)
    for gi, w in enumerate(POOL_WINDOWS):
        sl = slice(gi * POOL_GROUP, (gi + 1) * POOL_GROUP)
        tot = None
        for off in range(-(w // 2), w - w // 2):
            part = ue_ref[HALO + off:HALO + off + tm, sl]
            tot = part if tot is None else tot + part
        lo = jnp.maximum(pos - w // 2, 0)
        hi = jnp.minimum(pos + (w - w // 2), n_seq)
        cnt = (hi - lo).astype(F32)
        pooled = (tot / cnt - ue_ref[HALO:HALO + tm, sl]).astype(BF16)
        y_ref[0, :, sl] = (_dot(pooled, pw_ref[gi]) * ps_ref[:, sl]).astype(BF16)


def _ab_proj(x, mod, g, w_in, q_g, k_g, pool_w, pool_scale, cos_t, sin_t, *, tm):
    b, n, d = x.shape
    rope = cos_t is not None
    nq = A_HEADS * HEAD_DIM
    nkv = A_KV_HEADS * HEAD_DIM
    npool = POOL_GROUP * len(POOL_WINDOWS)
    mod_map = (lambda i, j: (i, 0, 0)) if mod.shape[0] == b else (lambda i, j: (0, 0, 0))
    in_specs = _halo_specs(n, tm, d) + [
        pl.BlockSpec((1, N_MOD, d), mod_map),
        _resident((1, d)),
        _resident(w_in.shape),
        _resident((1, HEAD_DIM)),
        _resident((1, HEAD_DIM)),
        _resident(pool_w.shape),
        _resident((1, npool)),
    ]
    args = [x, x, x, mod, g, w_in, q_g, k_g, pool_w, pool_scale]
    if rope:
        in_specs += [pl.BlockSpec((tm, HEAD_DIM), lambda i, j: (j, 0))] * 2
        args += [cos_t, sin_t]
    tile = lambda c: pl.BlockSpec((1, tm, c), lambda i, j: (i, j, 0))
    return pl.pallas_call(
        functools.partial(_ab_proj_kernel, tm=tm, n_seq=n, rope=rope),
        out_shape=(
            jax.ShapeDtypeStruct((b, n, nq), BF16),
            jax.ShapeDtypeStruct((b, n, nkv), BF16),
            jax.ShapeDtypeStruct((b, n, nkv), BF16),
            jax.ShapeDtypeStruct((b, n, npool), BF16),
        ),
        grid=(b, n // tm),
        in_specs=in_specs,
        out_specs=(tile(nq), tile(nkv), tile(nkv), tile(npool)),
        scratch_shapes=[pltpu.VMEM((tm + 2 * HALO, npool), F32)],
        compiler_params=_params(2),
        name="attn_pool_proj",
    )(*args)


def _rope_tables(n):
    rows = n // GRID_W
    row = jnp.repeat(jnp.arange(rows), GRID_W).astype(F32)
    col = jnp.tile(jnp.arange(GRID_W), rows).astype(F32)
    half = HEAD_DIM // 2
    inv_freq = jnp.power(ROPE_THETA, -jnp.arange(0, half, 2, dtype=F32) / half)
    ang_r = row[:, None] * inv_freq
    ang_c = col[:, None] * inv_freq
    cos_t = jnp.concatenate([jnp.cos(ang_r)] * 2 + [jnp.cos(ang_c)] * 2, axis=-1)
    sin_t = jnp.concatenate([-jnp.sin(ang_r), jnp.sin(ang_r), -jnp.sin(ang_c), jnp.sin(ang_c)], axis=-1)
    return cos_t, sin_t


def _attn_kernel(*refs, n_kv_sets, tq):
    q_ref = refs[0]
    kv_refs = refs[1:1 + 2 * n_kv_sets]
    o_ref = refs[-1]
    scale = HEAD_DIM ** -0.5
    group = A_HEADS // A_KV_HEADS
    for g in range(A_KV_HEADS):
        ksl = slice(g * HEAD_DIM, (g + 1) * HEAD_DIM)
        qq = jnp.concatenate(
            [q_ref[0, :, (g * group + i) * HEAD_DIM:(g * group + i + 1) * HEAD_DIM] for i in range(group)], axis=0)
        scores = [_dot_nt(qq, kv_refs[2 * s][0, :, ksl]) for s in range(n_kv_sets)]
        m = None
        for sc in scores:
            ms = jnp.max(sc, axis=-1, keepdims=True)
            m = ms if m is None else jnp.maximum(m, ms)
        den = None
        acc = None
        for s, sc in enumerate(scores):
            e = jnp.exp((sc - m) * scale)
            ds = jnp.sum(e, axis=-1, keepdims=True)
            den = ds if den is None else den + ds
            pv = _dot(e.astype(BF16), kv_refs[2 * s + 1][0, :, ksl])
            acc = pv if acc is None else acc + pv
        out = acc / den
        for i in range(group):
            hq = g * group + i
            o_ref[0, :, hq * HEAD_DIM:(hq + 1) * HEAD_DIM] = out[i * tq:(i + 1) * tq].astype(BF16)


def _attention(q, kv_sets, *, tq):
    b, n, nq = q.shape
    in_specs = [pl.BlockSpec((1, tq, nq), lambda i, j: (i, j, 0))]
    args = [q]
    for k, v in kv_sets:
        for t in (k, v):
            in_specs.append(pl.BlockSpec((1,) + t.shape[1:], lambda i, j: (i, 0, 0)))
            args.append(t)
    return pl.pallas_call(
        functools.partial(_attn_kernel, n_kv_sets=len(kv_sets), tq=tq),
        out_shape=jax.ShapeDtypeStruct(q.shape, BF16),
        grid=(b, n // tq),
        in_specs=in_specs,
        out_specs=pl.BlockSpec((1, tq, nq), lambda i, j: (i, j, 0)),
        compiler_params=_params(2),
        name="gqa_attention",
    )(*args)


def _segment_cumsum(y, seg, reverse):
    n = y.shape[0]
    r = lax.broadcasted_iota(jnp.int32, (n, 1), 0) % seg
    s = 1
    while s < seg:
        if reverse:
            y = y + jnp.where(r < seg - s, pltpu.roll(y, n - s, 0), 0.0)
        else:
            y = y + jnp.where(r >= s, pltpu.roll(y, s, 0), 0.0)
        s *= 2
    return y


def _gdn_proj_kernel(x_ref, xp_ref, xn_ref, mod_ref, g_ref, w_ref, cw_ref, al_ref, dt_ref,
                     q_ref, k_ref, v_ref, z_ref, gate_ref, pe_ref, *, tm):
    cdim = C_HEADS * HEAD_DIM
    mod = mod_ref[0]
    xe = _normed_ext_tile(x_ref, xp_ref, xn_ref, mod, g_ref[...])
    p = _dot(xe, w_ref[...])
    pe_ref[...] = _zero_invalid_halo(p[:, :3 * cdim], tm)
    pm = p[HALO:HALO + tm]
    z_ref[0] = pm[:, 3 * cdim:4 * cdim].astype(BF16)

    for part, out_ref in enumerate((q_ref, k_ref, v_ref)):
        for h in range(C_HEADS):
            c0 = part * cdim + h * HEAD_DIM
            csl = slice(c0, c0 + HEAD_DIM)
            t = (pe_ref[HALO - 1:HALO - 1 + tm, csl] * cw_ref[0:1, csl]
                 + pe_ref[HALO:HALO + tm, csl] * cw_ref[1:2, csl]
                 + pe_ref[HALO + 1:HALO + 1 + tm, csl] * cw_ref[2:3, csl])
            t = _silu(t)
            if part < 2:
                t = t * lax.rsqrt(jnp.sum(t * t, axis=-1, keepdims=True) + EPS)
            if part == 0:
                t = t * (HEAD_DIM ** -0.5)
            out_ref[0, :, h * HEAD_DIM:(h + 1) * HEAD_DIM] = t.astype(BF16)

    ab = pm[:, 4 * cdim:]
    col = lax.broadcasted_iota(jnp.int32, ab.shape, 1)
    xa = ab + dt_ref[...]
    softplus = jnp.maximum(xa, 0.0) + jnp.log(1.0 + jnp.exp(-jnp.abs(xa)))
    gdec = -jnp.exp(al_ref[...]) * softplus
    is_a = (col % (2 * C_HEADS)) < C_HEADS
    gdec = jnp.where(is_a, gdec, 0.0)
    fwd = _segment_cumsum(gdec, CHUNK, reverse=False)
    bwd = _segment_cumsum(gdec, CHUNK, reverse=True)
    cum = jnp.where(col < 2 * C_HEADS, fwd, bwd)
    gate_ref[0] = jnp.where(is_a, cum, _sigmoid(ab))


def _gdn_proj(x, mod, g, w_in, conv_w, a_log_cols, dt_cols, *, tm):
    b, n, d = x.shape
    cdim = C_HEADS * HEAD_DIM
    ngate = 4 * C_HEADS
    mod_map = (lambda i, j: (i, 0, 0)) if mod.shape[0] == b else (lambda i, j: (0, 0, 0))
    in_specs = _halo_specs(n, tm, d) + [
        pl.BlockSpec((1, N_MOD, d), mod_map),
        _resident((1, d)),
        _resident(w_in.shape),
        _resident(conv_w.shape),
        _resident((1, ngate)),
        _resident((1, ngate)),
    ]
    tile = lambda c: pl.BlockSpec((1, tm, c), lambda i, j: (i, j, 0))
    return pl.pallas_call(
        functools.partial(_gdn_proj_kernel, tm=tm),
        out_shape=tuple(jax.ShapeDtypeStruct((b, n, cdim), BF16) for _ in range(4))
        + (jax.ShapeDtypeStruct((b, n, ngate), F32),),
        grid=(b, n // tm),
        in_specs=in_specs,
        out_specs=(tile(cdim),) * 4 + (tile(ngate),),
        scratch_shapes=[pltpu.VMEM((tm + 2 * HALO, 3 * cdim), F32)],
        compiler_params=_params(2),
        name="gdn_proj",
    )(x, x, x, mod, g, w_in, conv_w, a_log_cols, dt_cols)


def _unit_triangular_solve(low, rhs, ri, ci):
    c = low.shape[0]
    eye = (ri == ci).astype(F32)

    def off_block(shift):
        return ((ri >> (shift + 1)) == (ci >> (shift + 1))) & ((ri >> shift) != (ci >> shift))

    inv = eye - jnp.where(off_block(0), low, 0.0)
    shift = 1
    while (1 << shift) < c:
        inv16 = inv.astype(BF16)
        mid = jnp.where(off_block(shift), low, 0.0).astype(BF16)
        inv = inv - _dot(_dot(inv16, mid).astype(BF16), inv16)
        shift += 1
    return _dot(inv.astype(BF16), rhs.astype(BF16))


def _gdn_chunk(q, k, v, gc_col, beta_col, gc_row, state, reverse, need_out):
    c = q.shape[0]
    ri = lax.broadcasted_iota(jnp.int32, (c, c), 0)
    ci = lax.broadcasted_iota(jnp.int32, (c, c), 1)
    incl = (ri <= ci) if reverse else (ri >= ci)
    strict = (ri < ci) if reverse else (ri > ci)
    decay = jnp.exp(jnp.where(incl, gc_col - gc_row, NEG_BIG))
    kb16 = k.astype(BF16)
    egc = jnp.exp(gc_col)
    low = jnp.where(strict, (beta_col * _dot_nt(kb16, kb16)) * decay, 0.0)
    rhs = jnp.concatenate([v * beta_col, k * (beta_col * egc)], axis=-1)
    sol = _unit_triangular_solve(low, rhs, ri, ci)
    dh = v.shape[-1]
    u, w = sol[:, :dh], sol[:, dh:]
    s16 = state.astype(BF16)
    v_new = u - _dot(w.astype(BF16), s16)
    vn16 = v_new.astype(BF16)
    last = gc_col[0:1] if reverse else gc_col[c - 1:c]
    k_tail = (k * jnp.exp(last - gc_col)).astype(BF16)
    new_state = state * jnp.exp(last) + _dot_tn(k_tail, vn16)
    if not need_out:
        return new_state, None
    qk = jnp.where(incl, _dot_nt(q.astype(BF16), kb16) * decay, 0.0)
    o = _dot((q * egc).astype(BF16), s16) + _dot(qk.astype(BF16), vn16)
    return new_state, o


def _gdn_kernel(qx_ref, kx_ref, vx_ref, zx_ref, gx_ref, gxt_ref, qh_ref, kh_ref, vh_ref, gh_ref, ght_ref,
                og_ref, y_ref, o_ref, st_ref, *, hg):
    ncx = qx_ref.shape[1] // CHUNK
    nch = qh_ref.shape[1] // CHUNK
    st_ref[...] = jnp.zeros(st_ref.shape, F32)
    o_ref[...] = jnp.zeros(o_ref.shape, F32)

    def run(q_ref, k_ref, v_ref, g_ref, gt_ref, nc, need_out):
        def body(i, carry):
            for d in range(2):
                ck = i if d == 0 else nc - 1 - i
                r0 = pl.multiple_of(ck * CHUNK, CHUNK)
                gcols = g_ref[0, 0, pl.ds(r0, CHUNK), :]
                grows = gt_ref[0, 0, ck]
                for h in range(hg):
                    hsl = slice(h * HEAD_DIM, (h + 1) * HEAD_DIM)
                    q = q_ref[0, pl.ds(r0, CHUNK), hsl].astype(F32)
                    k = k_ref[0, pl.ds(r0, CHUNK), hsl].astype(F32)
                    v = v_ref[0, pl.ds(r0, CHUNK), hsl].astype(F32)
                    cg = d * 2 * hg + h
                    cb = cg + hg
                    idx = d * hg + h
                    new_state, o = _gdn_chunk(q, k, v, gcols[:, cg:cg + 1], gcols[:, cb:cb + 1],
                                              grows[cg:cg + 1, :], st_ref[idx],
                                              reverse=(d == 1), need_out=need_out)
                    st_ref[idx] = new_state
                    if need_out:
                        o_ref[pl.ds(r0, CHUNK), hsl] += o
            return carry
        lax.fori_loop(0, nc, body, 0)

    run(qh_ref, kh_ref, vh_ref, gh_ref, ght_ref, nch, False)
    run(qx_ref, kx_ref, vx_ref, gx_ref, gxt_ref, ncx, True)

    for h in range(hg):
        hsl = slice(h * HEAD_DIM, (h + 1) * HEAD_DIM)
        o = o_ref[:, hsl]
        y = o * lax.rsqrt(jnp.mean(o * o, axis=-1, keepdims=True) + EPS) * og_ref[...]
        y_ref[0, :, hsl] = (y * _silu(zx_ref[0, :, hsl].astype(F32))).astype(BF16)


def _gdn(qx, kx, vx, zx, gx, qh, kh, vh, gh, o_g, *, hg):
    b, n, cdim = qx.shape
    nh = qh.shape[1]
    ng = C_HEADS // hg
    ncol = 4 * hg

    def regroup(gates, rows):
        t = gates.reshape(b, rows, 2, 2, ng, hg).transpose(0, 4, 1, 2, 3, 5).reshape(b, ng, rows, ncol)
        return t, t.reshape(b, ng, rows // CHUNK, CHUNK, ncol).transpose(0, 1, 2, 4, 3)

    gx, gxt = regroup(gx, n)
    gh, ght = regroup(gh, nh)
    wdt = hg * HEAD_DIM
    seq = lambda rows: pl.BlockSpec((1, rows, wdt), lambda i, j: (i, 0, j))
    gcol = lambda rows: pl.BlockSpec((1, 1, rows, ncol), lambda i, j: (i, j, 0, 0))
    grow = lambda rows: pl.BlockSpec((1, 1, rows // CHUNK, ncol, CHUNK), lambda i, j: (i, j, 0, 0, 0))
    return pl.pallas_call(
        functools.partial(_gdn_kernel, hg=hg),
        out_shape=jax.ShapeDtypeStruct((b, n, cdim), BF16),
        grid=(b, ng),
        in_specs=[seq(n), seq(n), seq(n), seq(n), gcol(n), grow(n),
                  seq(nh), seq(nh), seq(nh), gcol(nh), grow(nh), _resident((1, HEAD_DIM))],
        out_specs=seq(n),
        scratch_shapes=[pltpu.VMEM((n, wdt), F32), pltpu.VMEM((2 * hg, HEAD_DIM, HEAD_DIM), F32)],
        compiler_params=_params(2),
        name="gated_delta",
    )(qx, kx, vx, zx, gx, gxt, qh, kh, vh, gh, ght, o_g)


def _gate_columns(vals):
    zeros = jnp.zeros_like(vals[0])
    return jnp.concatenate([vals[0], zeros, vals[1], zeros])[None].astype(F32)


def kernel(x, c, ctx, c_ctx, w_mod, b_mod, norm_g, ffn_wg, ffn_wu, ffn_wd, ab_w_in, ab_q_norm, ab_k_norm,
           pool_w, pool_scale, ab_w_out, gdn_w_in, gdn_conv_w, gdn_a_log, gdn_dt_bias, gdn_o_norm, gdn_w_out):
    depth = w_mod.shape[0]
    b, n, d = x.shape
    n_ctx = ctx.shape[1]
    tm_x = min(512, n)
    tm_h = n_ctx

    pad = (-(b + 1)) % SUBLANES
    cond = jnp.concatenate([c, c_ctx[None], jnp.zeros((pad, d), F32)], axis=0)
    mods = _modulation(cond, w_mod, b_mod)
    cos_t, sin_t = _rope_tables(n)

    h = ctx
    for i in range(depth):
        last = i == depth - 1
        mx = mods[i, :b].reshape(b, N_MOD, d)
        mh = mods[i, b:b + 1].reshape(1, N_MOD, d)
        wg, wu, wd = (t[i].astype(BF16) for t in (ffn_wg, ffn_wu, ffn_wd))
        g0, g1, g2 = (norm_g[i, s][None] for s in range(3))
        x = _ffn(x, mx, g0, wg[0], wu[0], wd[0], s=0, tm=tm_x)
        h = _ffn(h, mh, g0, wg[0], wu[0], wd[0], s=0, tm=tm_h)
        j = i // 2
        if i % 2 == 0:
            w_in = ab_w_in[j].astype(BF16)
            pw = pool_w[j].astype(BF16)
            common = (w_in, ab_q_norm[j][None], ab_k_norm[j][None], pw, pool_scale[j][None])
            qx, kx, vx, px = _ab_proj(x, mx, g1, *common, cos_t, sin_t, tm=tm_x)
            qh, kh, vh, ph = _ab_proj(h, mh, g1, *common, None, None, tm=tm_h)
            ax = _attention(qx, [(kh, vh), (kx, vx)], tq=min(256, n))
            mix_x = (ax, px)
            w_out = ab_w_out[j].astype(BF16)
            if not last:
                ah = _attention(qh, [(kh, vh)], tq=min(256, n_ctx))
                mix_h = (ah, ph)
        else:
            w_in = gdn_w_in[j].astype(BF16)
            al = _gate_columns(gdn_a_log[j])
            dt = _gate_columns(gdn_dt_bias[j])
            qx, kx, vx, zx, gx = _gdn_proj(x, mx, g1, w_in, gdn_conv_w[j], al, dt, tm=tm_x)
            qh, kh, vh, zh, gh = _gdn_proj(h, mh, g1, w_in, gdn_conv_w[j], al, dt, tm=tm_h)
            if not last:
                raise NotImplementedError("context output of the DeltaNet mixer is only needed for depth > 2")
            mix_x = (_gdn(qx, kx, vx, zx, gx, qh, kh, vh, gh, gdn_o_norm[j][None], hg=2),)
            w_out = gdn_w_out[j].astype(BF16)
        x = _ffn(x, mx, g2, wg[1], wu[1], wd[1], s=2, tm=tm_x, mix=mix_x, w_out=w_out)
        if not last:
            h = _ffn(h, mh, g2, wg[1], wu[1], wd[1], s=2, tm=tm_h, mix=mix_h, w_out=w_out)
    return x
```

```python
import functools
import math

import jax
import jax.numpy as jnp
from jax import lax
from jax.experimental import pallas as pl
from jax.experimental.pallas import tpu as pltpu

F32 = jnp.float32
BF16 = jnp.bfloat16

EPS = 1e-6
HEAD_DIM = 128
GRID_W = 64
ROPE_THETA = 10000.0
N_MOD = 9
A_HEADS = 4
A_KV_HEADS = 2
POOL_WINDOWS = (2, 4, 8, 16)
POOL_GROUP = 128
C_HEADS = 8
CHUNK = 64

SUBLANES = 8
VMEM_LIMIT_BYTES = 56 * 1024 * 1024
HALO = SUBLANES

NEG_BIG = -1e30


def _params(n_grid, vmem=VMEM_LIMIT_BYTES):
    return pltpu.CompilerParams(dimension_semantics=("arbitrary",) * n_grid, vmem_limit_bytes=vmem)


def _resident(shape):
    nd = len(shape)
    return pl.BlockSpec(shape, lambda *_: (0,) * nd, pipeline_mode=pl.Buffered(1))


def _sigmoid(t):
    return 1.0 / (1.0 + jnp.exp(-t))


def _silu(t):
    return t * _sigmoid(t)


def _rms_mod(x, g, shift, scale):
    y = x * lax.rsqrt(jnp.mean(x * x, axis=-1, keepdims=True) + EPS)
    return (y * g) * (1.0 + scale) + shift


def _dot(a, b):
    return jnp.dot(a, b, preferred_element_type=F32)


def _dot_nt(a, b):
    return lax.dot_general(a, b, (((1,), (1,)), ((), ())), preferred_element_type=F32)


def _dot_tn(a, b):
    return lax.dot_general(a, b, (((0,), (0,)), ((), ())), preferred_element_type=F32)


def _mod_kernel(c_ref, w_ref, b_ref, o_ref):
    a = _silu(c_ref[...]).astype(BF16)
    o_ref[0] = _dot(a, w_ref[0].astype(BF16)) + b_ref[0]


def _modulation(cond, w_mod, b_mod):
    depth, d, nd = w_mod.shape
    rows = cond.shape[0]
    bn = d
    return pl.pallas_call(
        _mod_kernel,
        out_shape=jax.ShapeDtypeStruct((depth, rows, nd), F32),
        grid=(depth, nd // bn),
        in_specs=[
            pl.BlockSpec((rows, d), lambda i, j: (0, 0)),
            pl.BlockSpec((1, d, bn), lambda i, j: (i, 0, j)),
            pl.BlockSpec((1, 1, bn), lambda i, j: (i, 0, j)),
        ],
        out_specs=pl.BlockSpec((1, rows, bn), lambda i, j: (i, 0, j)),
        compiler_params=_params(2),
        name="adaln_mod",
    )(cond, w_mod, b_mod.reshape(depth, 1, nd))


def _ffn_kernel(*refs, s, n_mix):
    x_ref, mod_ref, g_ref, wg_ref, wu_ref, wd_ref = refs[:6]
    mix_refs = refs[6:6 + n_mix]
    wo_ref = refs[6 + n_mix] if n_mix else None
    o_ref = refs[-1]
    x = x_ref[0]
    mod = mod_ref[0]
    if n_mix:
        y = None
        k0 = 0
        for m_ref in mix_refs:
            kk = m_ref.shape[-1]
            part = _dot(m_ref[0], wo_ref[k0:k0 + kk, :])
            y = part if y is None else y + part
            k0 += kk
        x = x + mod[5:6] * y
    xn = _rms_mod(x, g_ref[...], mod[3 * s:3 * s + 1], mod[3 * s + 1:3 * s + 2]).astype(BF16)
    gate = _dot(xn, wg_ref[...])
    up = _dot(xn, wu_ref[...])
    hid = (_silu(gate) * up).astype(BF16)
    o_ref[0] = x + (0.5 * mod[3 * s + 2:3 * s + 3]) * _dot(hid, wd_ref[...])


def _ffn(x, mod, g, wg, wu, wd, *, s, tm, mix=(), w_out=None):
    b, n, d = x.shape
    f = wg.shape[1]
    mod_map = (lambda i, j: (i, 0, 0)) if mod.shape[0] == b else (lambda i, j: (0, 0, 0))
    in_specs = [
        pl.BlockSpec((1, tm, d), lambda i, j: (i, j, 0)),
        pl.BlockSpec((1, N_MOD, d), mod_map),
        _resident((1, d)),
        _resident((d, f)),
        _resident((d, f)),
        _resident((f, d)),
    ]
    args = [x, mod, g, wg, wu, wd]
    for m in mix:
        in_specs.append(pl.BlockSpec((1, tm, m.shape[-1]), lambda i, j: (i, j, 0)))
        args.append(m)
    if mix:
        in_specs.append(_resident(w_out.shape))
        args.append(w_out)
    return pl.pallas_call(
        functools.partial(_ffn_kernel, s=s, n_mix=len(mix)),
        out_shape=jax.ShapeDtypeStruct(x.shape, F32),
        grid=(b, n // tm),
        in_specs=in_specs,
        out_specs=pl.BlockSpec((1, tm, d), lambda i, j: (i, j, 0)),
        compiler_params=_params(2),
        name="swiglu_half_step",
    )(*args)


def _halo_specs(n, tm, d):
    per = tm // HALO
    last = n // HALO - 1
    return [
        pl.BlockSpec((1, tm, d), lambda i, j: (i, j, 0)),
        pl.BlockSpec((1, HALO, d), lambda i, j: (i, jnp.maximum(j * per - 1, 0), 0)),
        pl.BlockSpec((1, HALO, d), lambda i, j: (i, jnp.minimum((j + 1) * per, last), 0)),
    ]


def _normed_ext_tile(x_ref, xp_ref, xn_ref, mod, g):
    xe = jnp.concatenate([xp_ref[0], x_ref[0], xn_ref[0]], axis=0)
    return _rms_mod(xe, g, mod[3:4], mod[4:5]).astype(BF16)


def _zero_invalid_halo(pe, tm):
    j = pl.program_id(1)
    row = lax.broadcasted_iota(jnp.int32, (tm + 2 * HALO, 1), 0)
    dead = ((row < HALO) & (j == 0)) | ((row >= tm + HALO) & (j == pl.num_programs(1) - 1))
    return jnp.where(dead, 0.0, pe)


def _swap32(t):
    lane = lax.broadcasted_iota(jnp.int32, t.shape, 1)
    return jnp.where((lane % 64) < 32, pltpu.roll(t, HEAD_DIM - 32, 1), pltpu.roll(t, 32, 1))


def _ab_proj_kernel(*refs, tm, n_seq, rope):
    x_ref, xp_ref, xn_ref, mod_ref, g_ref, w_ref, qg_ref, kg_ref, pw_ref, ps_ref = refs[:10]
    if rope:
        cos_ref, sin_ref = refs[10:12]
    q_ref, k_ref, v_ref, y_ref, ue_ref = refs[-5:]
    mod = mod_ref[0]
    xe = _normed_ext_tile(x_ref, xp_ref, xn_ref, mod, g_ref[...])
    p = _dot(xe, w_ref[...])
    pm = p[HALO:HALO + tm]
    nq = A_HEADS * HEAD_DIM
    nkv = A_KV_HEADS * HEAD_DIM

    def head_norm(t, gain):
        t = t * lax.rsqrt(jnp.mean(t * t, axis=-1, keepdims=True) + EPS) * gain
        if rope:
            t = t * cos_ref[...] + _swap32(t) * sin_ref[...]
        return t.astype(BF16)

    for h in range(A_HEADS):
        sl = slice(h * HEAD_DIM, (h + 1) * HEAD_DIM)
        q_ref[0, :, sl] = head_norm(pm[:, sl], qg_ref[...])
    for h in range(A_KV_HEADS):
        sl = slice(h * HEAD_DIM, (h + 1) * HEAD_DIM)
        k_ref[0, :, sl] = head_norm(pm[:, nq + h * HEAD_DIM:nq + (h + 1) * HEAD_DIM], kg_ref[...])
    v_ref[0] = pm[:, nq + nkv:nq + 2 * nkv].astype(BF16)

    u0 = nq + 2 * nkv
    ue_ref[...] = _zero_invalid_halo(p[:, u0:], tm)
    pos = pl.program_id(1) * tm + lax.broadcasted_iota(jnp.int32, (tm, 1), 0)
    for gi, w in enumerate(POOL_WINDOWS):
        sl = slice(gi * POOL_GROUP, (gi + 1) * POOL_GROUP)
        tot = None
        for off in range(-(w // 2), w - w // 2):
            part = ue_ref[HALO + off:HALO + off + tm, sl]
            tot = part if tot is None else tot + part
        lo = jnp.maximum(pos - w // 2, 0)
        hi = jnp.minimum(pos + (w - w // 2), n_seq)
        cnt = (hi - lo).astype(F32)
        pooled = (tot / cnt - ue_ref[HALO:HALO + tm, sl]).astype(BF16)
        y_ref[0, :, sl] = (_dot(pooled, pw_ref[gi]) * ps_ref[:, sl]).astype(BF16)


def _ab_proj(x, mod, g, w_in, q_g, k_g, pool_w, pool_scale, cos_t, sin_t, *, tm):
    b, n, d = x.shape
    rope = cos_t is not None
    nq = A_HEADS * HEAD_DIM
    nkv = A_KV_HEADS * HEAD_DIM
    npool = POOL_GROUP * len(POOL_WINDOWS)
    mod_map = (lambda i, j: (i, 0, 0)) if mod.shape[0] == b else (lambda i, j: (0, 0, 0))
    in_specs = _halo_specs(n, tm, d) + [
        pl.BlockSpec((1, N_MOD, d), mod_map),
        _resident((1, d)),
        _resident(w_in.shape),
        _resident((1, HEAD_DIM)),
        _resident((1, HEAD_DIM)),
        _resident(pool_w.shape),
        _resident((1, npool)),
    ]
    args = [x, x, x, mod, g, w_in, q_g, k_g, pool_w, pool_scale]
    if rope:
        in_specs += [pl.BlockSpec((tm, HEAD_DIM), lambda i, j: (j, 0))] * 2
        args += [cos_t, sin_t]
    tile = lambda c: pl.BlockSpec((1, tm, c), lambda i, j: (i, j, 0))
    return pl.pallas_call(
        functools.partial(_ab_proj_kernel, tm=tm, n_seq=n, rope=rope),
        out_shape=(
            jax.ShapeDtypeStruct((b, n, nq), BF16),
            jax.ShapeDtypeStruct((b, n, nkv), BF16),
            jax.ShapeDtypeStruct((b, n, nkv), BF16),
            jax.ShapeDtypeStruct((b, n, npool), BF16),
        ),
        grid=(b, n // tm),
        in_specs=in_specs,
        out_specs=(tile(nq), tile(nkv), tile(nkv), tile(npool)),
        scratch_shapes=[pltpu.VMEM((tm + 2 * HALO, npool), F32)],
        compiler_params=_params(2),
        name="attn_pool_proj",
    )(*args)


def _rope_tables(n):
    rows = n // GRID_W
    row = jnp.repeat(jnp.arange(rows), GRID_W).astype(F32)
    col = jnp.tile(jnp.arange(GRID_W), rows).astype(F32)
    half = HEAD_DIM // 2
    inv_freq = jnp.power(ROPE_THETA, -jnp.arange(0, half, 2, dtype=F32) / half)
    ang_r = row[:, None] * inv_freq
    ang_c = col[:, None] * inv_freq
    cos_t = jnp.concatenate([jnp.cos(ang_r)] * 2 + [jnp.cos(ang_c)] * 2, axis=-1)
    sin_t = jnp.concatenate([-jnp.sin(ang_r), jnp.sin(ang_r), -jnp.sin(ang_c), jnp.sin(ang_c)], axis=-1)
    return cos_t, sin_t


def _attn_kernel(*refs, n_kv_sets, tq):
    q_ref = refs[0]
    kv_refs = refs[1:1 + 2 * n_kv_sets]
    o_ref = refs[-1]
    scale = HEAD_DIM ** -0.5
    group = A_HEADS // A_KV_HEADS
    for g in range(A_KV_HEADS):
        ksl = slice(g * HEAD_DIM, (g + 1) * HEAD_DIM)
        qq = jnp.concatenate(
            [q_ref[0, :, (g * group + i) * HEAD_DIM:(g * group + i + 1) * HEAD_DIM] for i in range(group)], axis=0)
        scores = [_dot_nt(qq, kv_refs[2 * s][0, :, ksl]) for s in range(n_kv_sets)]
        m = None
        for sc in scores:
            ms = jnp.max(sc, axis=-1, keepdims=True)
            m = ms if m is None else jnp.maximum(m, ms)
        den = None
        acc = None
        for s, sc in enumerate(scores):
            e = jnp.exp((sc - m) * scale)
            ds = jnp.sum(e, axis=-1, keepdims=True)
            den = ds if den is None else den + ds
            pv = _dot(e.astype(BF16), kv_refs[2 * s + 1][0, :, ksl])
            acc = pv if acc is None else acc + pv
        out = acc / den
        for i in range(group):
            hq = g * group + i
            o_ref[0, :, hq * HEAD_DIM:(hq + 1) * HEAD_DIM] = out[i * tq:(i + 1) * tq].astype(BF16)


def _attention(q, kv_sets, *, tq):
    b, n, nq = q.shape
    in_specs = [pl.BlockSpec((1, tq, nq), lambda i, j: (i, j, 0))]
    args = [q]
    for k, v in kv_sets:
        for t in (k, v):
            in_specs.append(pl.BlockSpec((1,) + t.shape[1:], lambda i, j: (i, 0, 0)))
            args.append(t)
    return pl.pallas_call(
        functools.partial(_attn_kernel, n_kv_sets=len(kv_sets), tq=tq),
        out_shape=jax.ShapeDtypeStruct(q.shape, BF16),
        grid=(b, n // tq),
        in_specs=in_specs,
        out_specs=pl.BlockSpec((1, tq, nq), lambda i, j: (i, j, 0)),
        compiler_params=_params(2),
        name="gqa_attention",
    )(*args)


def _segment_cumsum(y, seg, reverse):
    n = y.shape[0]
    r = lax.broadcasted_iota(jnp.int32, (n, 1), 0) % seg
    s = 1
    while s < seg:
        if reverse:
            y = y + jnp.where(r < seg - s, pltpu.roll(y, n - s, 0), 0.0)
        else:
            y = y + jnp.where(r >= s, pltpu.roll(y, s, 0), 0.0)
        s *= 2
    return y


def _gdn_proj_kernel(x_ref, xp_ref, xn_ref, mod_ref, g_ref, w_ref, cw_ref, al_ref, dt_ref,
                     q_ref, k_ref, v_ref, z_ref, gate_ref, pe_ref, *, tm):
    cdim = C_HEADS * HEAD_DIM
    mod = mod_ref[0]
    xe = _normed_ext_tile(x_ref, xp_ref, xn_ref, mod, g_ref[...])
    p = _dot(xe, w_ref[...])
    pe_ref[...] = _zero_invalid_halo(p[:, :3 * cdim], tm)
    pm = p[HALO:HALO + tm]
    z_ref[0] = pm[:, 3 * cdim:4 * cdim].astype(BF16)

    for part, out_ref in enumerate((q_ref, k_ref, v_ref)):
        for h in range(C_HEADS):
            c0 = part * cdim + h * HEAD_DIM
            csl = slice(c0, c0 + HEAD_DIM)
            t = (pe_ref[HALO - 1:HALO - 1 + tm, csl] * cw_ref[0:1, csl]
                 + pe_ref[HALO:HALO + tm, csl] * cw_ref[1:2, csl]
                 + pe_ref[HALO + 1:HALO + 1 + tm, csl] * cw_ref[2:3, csl])
            t = _silu(t)
            if part < 2:
                t = t * lax.rsqrt(jnp.sum(t * t, axis=-1, keepdims=True) + EPS)
            if part == 0:
                t = t * (HEAD_DIM ** -0.5)
            out_ref[0, :, h * HEAD_DIM:(h + 1) * HEAD_DIM] = t.astype(BF16)

    ab = pm[:, 4 * cdim:]
    col = lax.broadcasted_iota(jnp.int32, ab.shape, 1)
    xa = ab + dt_ref[...]
    softplus = jnp.maximum(xa, 0.0) + jnp.log(1.0 + jnp.exp(-jnp.abs(xa)))
    gdec = -jnp.exp(al_ref[...]) * softplus
    is_a = (col % (2 * C_HEADS)) < C_HEADS
    gdec = jnp.where(is_a, gdec, 0.0)
    fwd = _segment_cumsum(gdec, CHUNK, reverse=False)
    bwd = _segment_cumsum(gdec, CHUNK, reverse=True)
    cum = jnp.where(col < 2 * C_HEADS, fwd, bwd)
    gate_ref[0] = jnp.where(is_a, cum, _sigmoid(ab))


def _gdn_proj(x, mod, g, w_in, conv_w, a_log_cols, dt_cols, *, tm):
    b, n, d = x.shape
    cdim = C_HEADS * HEAD_DIM
    ngate = 4 * C_HEADS
    mod_map = (lambda i, j: (i, 0, 0)) if mod.shape[0] == b else (lambda i, j: (0, 0, 0))
    in_specs = _halo_specs(n, tm, d) + [
        pl.BlockSpec((1, N_MOD, d), mod_map),
        _resident((1, d)),
        _resident(w_in.shape),
        _resident(conv_w.shape),
        _resident((1, ngate)),
        _resident((1, ngate)),
    ]
    tile = lambda c: pl.BlockSpec((1, tm, c), lambda i, j: (i, j, 0))
    return pl.pallas_call(
        functools.partial(_gdn_proj_kernel, tm=tm),
        out_shape=tuple(jax.ShapeDtypeStruct((b, n, cdim), BF16) for _ in range(4))
        + (jax.ShapeDtypeStruct((b, n, ngate), F32),),
        grid=(b, n // tm),
        in_specs=in_specs,
        out_specs=(tile(cdim),) * 4 + (tile(ngate),),
        scratch_shapes=[pltpu.VMEM((tm + 2 * HALO, 3 * cdim), F32)],
        compiler_params=_params(2),
        name="gdn_proj",
    )(x, x, x, mod, g, w_in, conv_w, a_log_cols, dt_cols)


GDN_HG = 2
GDN_CHAINS = 2 * GDN_HG
GDN_PACK = GDN_CHAINS * CHUNK
GDN_PREP_UNROLL = 8


def _chunk_rows(start):
    return pl.ds(start if isinstance(start, int) else pl.multiple_of(start, CHUNK), CHUNK)


def _bf16_mask(cond):
    return jnp.where(cond, 1.0, 0.0).astype(BF16)


def _gdn_prepare(q_ref, k_ref, v_ref, gc_ref, gr_ref, u_ref, w_ref, qd_ref, kt_ref, qk_ref, row0, cks):
    c = CHUNK
    ri = lax.broadcasted_iota(jnp.int32, (c, GDN_PACK), 0)
    li = lax.broadcasted_iota(jnp.int32, (c, GDN_PACK), 1)
    cj = li & (c - 1)
    blk = li >> 6
    rev = li >= GDN_HG * c
    hi_idx = jnp.where(rev, cj, ri)
    lo_idx = jnp.where(rev, ri, cj)
    incl = hi_idx >= lo_idx
    strict = hi_idx > lo_idx
    eye = jnp.where(ri == cj, 1.0, 0.0)
    bi = lax.broadcasted_iota(jnp.int32, (GDN_PACK, GDN_PACK), 0) >> 6
    bj = lax.broadcasted_iota(jnp.int32, (GDN_PACK, GDN_PACK), 1) >> 6
    bd_mask = _bf16_mask(bi == bj)
    chain_mask = [_bf16_mask(blk == a) for a in range(GDN_CHAINS)]
    head_lane = lax.broadcasted_iota(jnp.int32, (c, GDN_HG * HEAD_DIM), 1) >> 7

    def off_block(shift):
        return ((ri >> (shift + 1)) == (cj >> (shift + 1))) & ((ri >> shift) != (cj >> shift))

    def block_diag(p16):
        return jnp.concatenate([p16] * GDN_CHAINS, axis=0) * bd_mask

    def packed_col(cols, first):
        out = None
        for a in range(GDN_CHAINS):
            d, h = divmod(a, GDN_HG)
            j = d * 2 * GDN_HG + first + h
            col = jnp.broadcast_to(cols[:, j:j + 1], (c, GDN_PACK))
            out = col if out is None else jnp.where(blk == a, col, out)
        return out

    n = len(cks)
    rows = [_chunk_rows(ck * c) for ck in cks]
    srows = [_chunk_rows(row0 + ck * c) for ck in cks]
    k16 = [k_ref[0, r, :] for r in rows]
    q16 = [q_ref[0, r, :] for r in rows]
    v16 = [v_ref[0, r, :] for r in rows]
    gcol = [gc_ref[0, 0, r, :] for r in rows]
    grow = [gr_ref[0, 0, pl.ds(ck, 1), :] for ck in cks]

    decay = [jnp.exp(jnp.where(incl, packed_col(gcol[i], 0) - grow[i], NEG_BIG)) for i in range(n)]
    kdiag = [jnp.concatenate([jnp.where(head_lane == h, k16[i], jnp.zeros_like(k16[i])) for h in range(GDN_HG)], axis=0)
             for i in range(n)]
    gram = [_dot_nt(jnp.concatenate([k16[i], q16[i]], axis=0), kdiag[i]) for i in range(n)]
    kk = [jnp.concatenate([gram[i][:c]] * 2, axis=1) for i in range(n)]
    qk = [jnp.concatenate([gram[i][c:]] * 2, axis=1) for i in range(n)]
    low = [jnp.where(strict, packed_col(gcol[i], GDN_HG) * kk[i] * decay[i], 0.0) for i in range(n)]
    qkd = [jnp.where(incl, qk[i] * decay[i], 0.0).astype(BF16) for i in range(n)]
    for i in range(n):
        for a in range(GDN_CHAINS):
            qk_ref[a, srows[i], :] = qkd[i][:, a * c:(a + 1) * c]

    inv = [eye - jnp.where(off_block(0), low[i], 0.0) for i in range(n)]
    shift = 1
    while (1 << shift) < c:
        inv16 = [t.astype(BF16) for t in inv]
        mid = [block_diag(jnp.where(off_block(shift), low[i], 0.0).astype(BF16)) for i in range(n)]
        tmp = [_dot(inv16[i], mid[i]).astype(BF16) for i in range(n)]
        inv = [inv[i] - _dot(tmp[i], block_diag(inv16[i])) for i in range(n)]
        shift += 1
    inv16 = [t.astype(BF16) for t in inv]

    for i in range(n):
        rhs = []
        for a in range(GDN_CHAINS):
            d, h = divmod(a, GDN_HG)
            jg = d * 2 * GDN_HG + h
            hsl = slice(h * HEAD_DIM, (h + 1) * HEAD_DIM)
            gc_a = gcol[i][:, jg:jg + 1]
            beta_a = gcol[i][:, jg + GDN_HG:jg + GDN_HG + 1]
            egc = jnp.exp(gc_a)
            kf = k16[i][:, hsl].astype(F32)
            rhs.append(jnp.concatenate([v16[i][:, hsl].astype(F32) * beta_a, kf * (beta_a * egc)], axis=1).astype(BF16))
            last = gc_a[0:1] if d == 1 else gc_a[c - 1:c]
            qd_ref[a, srows[i], :] = (q16[i][:, hsl].astype(F32) * egc).astype(BF16)
            kt_ref[a, srows[i], :] = (kf * jnp.exp(last - gc_a)).astype(BF16)
        rhs = jnp.concatenate(rhs, axis=0)
        for a in range(GDN_CHAINS):
            sol = _dot(inv16[i] * chain_mask[a], rhs)
            u_ref[a, srows[i], :] = sol[:, :HEAD_DIM]
            w_ref[a, srows[i], :] = sol[:, HEAD_DIM:].astype(BF16)


def _gdn_scan_step(gc_ref, u_ref, w_ref, qd_ref, kt_ref, qk_ref, o_ref, st_ref, row0, nc, i, need_out):
    c = CHUNK
    cks = [i if a < GDN_HG else nc - 1 - i for a in range(GDN_CHAINS)]
    srows = [_chunk_rows(row0 + ck * c) for ck in cks]
    state = [st_ref[a] for a in range(GDN_CHAINS)]
    s16 = [s.astype(BF16) for s in state]
    if need_out:
        lhs = [jnp.concatenate([w_ref[a, srows[a], :], qd_ref[a, srows[a], :]], axis=0) for a in range(GDN_CHAINS)]
    else:
        lhs = [w_ref[a, srows[a], :] for a in range(GDN_CHAINS)]
    ws = [_dot(lhs[a], s16[a]) for a in range(GDN_CHAINS)]
    vn16 = [(u_ref[a, srows[a], :] - ws[a][:c]).astype(BF16) for a in range(GDN_CHAINS)]
    upd = [_dot_tn(kt_ref[a, srows[a], :], vn16[a]) for a in range(GDN_CHAINS)]
    for a in range(GDN_CHAINS):
        d, h = divmod(a, GDN_HG)
        jg = d * 2 * GDN_HG + h
        last_row = cks[a] * c + (0 if d == 1 else c - 1)
        g_tot = jnp.exp(gc_ref[0, 0, pl.ds(last_row, 1), :][:, jg:jg + 1])
        st_ref[a] = state[a] * g_tot + upd[a]
    if need_out:
        out = [ws[a][c:] + _dot(qk_ref[a, srows[a], :], vn16[a]) for a in range(GDN_CHAINS)]
        for a in range(GDN_CHAINS):
            d, h = divmod(a, GDN_HG)
            orow = _chunk_rows(cks[a] * c)
            o_ref[d, orow, h * HEAD_DIM:(h + 1) * HEAD_DIM] = out[a]


def _gdn_kernel(qx_ref, kx_ref, vx_ref, zx_ref, gxc_ref, gxr_ref, qh_ref, kh_ref, vh_ref, ghc_ref, ghr_ref,
                og_ref, y_ref, u_ref, w_ref, qd_ref, kt_ref, qk_ref, o_ref, st_ref):
    nx = qx_ref.shape[1]
    nh = qh_ref.shape[1]
    ncx = nx // CHUNK
    nch = nh // CHUNK
    scratch = (u_ref, w_ref, qd_ref, kt_ref, qk_ref)

    _gdn_prepare(qh_ref, kh_ref, vh_ref, ghc_ref, ghr_ref, *scratch, 0, list(range(nch)))

    def prep_body(it, carry):
        cks = [it * GDN_PREP_UNROLL + j for j in range(GDN_PREP_UNROLL)]
        _gdn_prepare(qx_ref, kx_ref, vx_ref, gxc_ref, gxr_ref, *scratch, nh, cks)
        return carry
    lax.fori_loop(0, ncx // GDN_PREP_UNROLL, prep_body, 0)

    st_ref[...] = jnp.zeros(st_ref.shape, F32)

    def ctx_body(i, carry):
        _gdn_scan_step(ghc_ref, *scratch, o_ref, st_ref, 0, nch, i, False)
        return carry
    lax.fori_loop(0, nch, ctx_body, 0)

    def x_body(i, carry):
        _gdn_scan_step(gxc_ref, *scratch, o_ref, st_ref, nh, ncx, i, True)
        return carry
    lax.fori_loop(0, ncx, x_body, 0)

    for h in range(GDN_HG):
        hsl = slice(h * HEAD_DIM, (h + 1) * HEAD_DIM)
        o = o_ref[0, :, hsl] + o_ref[1, :, hsl]
        y = o * lax.rsqrt(jnp.mean(o * o, axis=-1, keepdims=True) + EPS) * og_ref[...]
        y_ref[0, :, hsl] = (y * _silu(zx_ref[0, :, hsl].astype(F32))).astype(BF16)


def _gdn(qx, kx, vx, zx, gx, qh, kh, vh, gh, o_g):
    b, n, cdim = qx.shape
    nh = qh.shape[1]
    hg = GDN_HG
    ng = C_HEADS // hg
    ncol = 4 * hg
    assert n % (CHUNK * GDN_PREP_UNROLL) == 0 and nh % CHUNK == 0 and CHUNK == 64

    def regroup(gates, rows):
        t = gates.reshape(b, rows, 2, 2, ng, hg)
        cols = t.transpose(0, 4, 1, 2, 3, 5).reshape(b, ng, rows, ncol)
        gc = t[:, :, :, 0].reshape(b, rows // CHUNK, CHUNK, 2, ng, hg)
        return cols, gc.transpose(0, 4, 1, 3, 5, 2).reshape(b, ng, rows // CHUNK, GDN_PACK)

    gxc, gxr = regroup(gx, n)
    ghc, ghr = regroup(gh, nh)
    wdt = hg * HEAD_DIM
    tot = n + nh
    seq = lambda rows: pl.BlockSpec((1, rows, wdt), lambda i, j: (i, 0, j))
    gcol = lambda rows: pl.BlockSpec((1, 1, rows, ncol), lambda i, j: (i, j, 0, 0))
    grow = lambda rows: pl.BlockSpec((1, 1, rows // CHUNK, GDN_PACK), lambda i, j: (i, j, 0, 0))
    return pl.pallas_call(
        _gdn_kernel,
        out_shape=jax.ShapeDtypeStruct((b, n, cdim), BF16),
        grid=(b, ng),
        in_specs=[seq(n), seq(n), seq(n), seq(n), gcol(n), grow(n),
                  seq(nh), seq(nh), seq(nh), gcol(nh), grow(nh), _resident((1, HEAD_DIM))],
        out_specs=seq(n),
        scratch_shapes=[
            pltpu.VMEM((GDN_CHAINS, tot, HEAD_DIM), F32),
            pltpu.VMEM((GDN_CHAINS, tot, HEAD_DIM), BF16),
            pltpu.VMEM((GDN_CHAINS, tot, HEAD_DIM), BF16),
            pltpu.VMEM((GDN_CHAINS, tot, HEAD_DIM), BF16),
            pltpu.VMEM((GDN_CHAINS, tot, CHUNK), BF16),
            pltpu.VMEM((2, n, wdt), F32),
            pltpu.VMEM((GDN_CHAINS, HEAD_DIM, HEAD_DIM), F32),
        ],
        compiler_params=_params(2),
        name="gated_delta",
    )(qx, kx, vx, zx, gxc, gxr, qh, kh, vh, ghc, ghr, o_g)


def _gate_columns(vals):
    zeros = jnp.zeros_like(vals[0])
    return jnp.concatenate([vals[0], zeros, vals[1], zeros])[None].astype(F32)


def kernel(x, c, ctx, c_ctx, w_mod, b_mod, norm_g, ffn_wg, ffn_wu, ffn_wd, ab_w_in, ab_q_norm, ab_k_norm,
           pool_w, pool_scale, ab_w_out, gdn_w_in, gdn_conv_w, gdn_a_log, gdn_dt_bias, gdn_o_norm, gdn_w_out):
    depth = w_mod.shape[0]
    b, n, d = x.shape
    n_ctx = ctx.shape[1]
    tm_x = min(512, n)
    tm_h = n_ctx

    pad = (-(b + 1)) % SUBLANES
    cond = jnp.concatenate([c, c_ctx[None], jnp.zeros((pad, d), F32)], axis=0)
    mods = _modulation(cond, w_mod, b_mod)
    cos_t, sin_t = _rope_tables(n)

    h = ctx
    for i in range(depth):
        last = i == depth - 1
        mx = mods[i, :b].reshape(b, N_MOD, d)
        mh = mods[i, b:b + 1].reshape(1, N_MOD, d)
        wg, wu, wd = (t[i].astype(BF16) for t in (ffn_wg, ffn_wu, ffn_wd))
        g0, g1, g2 = (norm_g[i, s][None] for s in range(3))
        x = _ffn(x, mx, g0, wg[0], wu[0], wd[0], s=0, tm=tm_x)
        h = _ffn(h, mh, g0, wg[0], wu[0], wd[0], s=0, tm=tm_h)
        j = i // 2
        if i % 2 == 0:
            w_in = ab_w_in[j].astype(BF16)
            pw = pool_w[j].astype(BF16)
            common = (w_in, ab_q_norm[j][None], ab_k_norm[j][None], pw, pool_scale[j][None])
            qx, kx, vx, px = _ab_proj(x, mx, g1, *common, cos_t, sin_t, tm=tm_x)
            qh, kh, vh, ph = _ab_proj(h, mh, g1, *common, None, None, tm=tm_h)
            ax = _attention(qx, [(kh, vh), (kx, vx)], tq=min(256, n))
            mix_x = (ax, px)
            w_out = ab_w_out[j].astype(BF16)
            if not last:
                ah = _attention(qh, [(kh, vh)], tq=min(256, n_ctx))
                mix_h = (ah, ph)
        else:
            w_in = gdn_w_in[j].astype(BF16)
            al = _gate_columns(gdn_a_log[j])
            dt = _gate_columns(gdn_dt_bias[j])
            qx, kx, vx, zx, gx = _gdn_proj(x, mx, g1, w_in, gdn_conv_w[j], al, dt, tm=tm_x)
            qh, kh, vh, zh, gh = _gdn_proj(h, mh, g1, w_in, gdn_conv_w[j], al, dt, tm=tm_h)
            if not last:
                raise NotImplementedError("context output of the DeltaNet mixer is only needed for depth > 2")
            mix_x = (_gdn(qx, kx, vx, zx, gx, qh, kh, vh, gh, gdn_o_norm[j][None]),)
            w_out = gdn_w_out[j].astype(BF16)
        x = _ffn(x, mx, g2, wg[1], wu[1], wd[1], s=2, tm=tm_x, mix=mix_x, w_out=w_out)
        if not last:
            h = _ffn(h, mh, g2, wg[1], wu[1], wd[1], s=2, tm=tm_h, mix=mix_h, w_out=w_out)
    return x
```

```python
import functools
import math

import jax
import jax.numpy as jnp
from jax import lax
from jax.experimental import pallas as pl
from jax.experimental.pallas import tpu as pltpu

F32 = jnp.float32
BF16 = jnp.bfloat16

EPS = 1e-6
HEAD_DIM = 128
GRID_W = 64
ROPE_THETA = 10000.0
N_MOD = 9
A_HEADS = 4
A_KV_HEADS = 2
POOL_WINDOWS = (2, 4, 8, 16)
POOL_GROUP = 128
C_HEADS = 8
CHUNK = 64

SUBLANES = 8
MXU_COLS = 256
VMEM_LIMIT_BYTES = 56 * 1024 * 1024
HALO = SUBLANES

NEG_BIG = -1e30


def _params(n_grid, vmem=VMEM_LIMIT_BYTES):
    return pltpu.CompilerParams(dimension_semantics=("arbitrary",) * n_grid, vmem_limit_bytes=vmem)


def _resident(shape):
    nd = len(shape)
    return pl.BlockSpec(shape, lambda *_: (0,) * nd, pipeline_mode=pl.Buffered(1))


def _sigmoid(t):
    return 1.0 / (1.0 + jnp.exp(-t))


def _silu(t):
    h = 0.5 * t
    return h + h * jnp.tanh(h)


def _rms_mod(x, g, shift, scale):
    y = x * lax.rsqrt(jnp.mean(x * x, axis=-1, keepdims=True) + EPS)
    return (y * g) * (1.0 + scale) + shift


def _dot(a, b):
    return jnp.dot(a, b, preferred_element_type=F32)


def _dot_nt(a, b):
    return lax.dot_general(a, b, (((1,), (1,)), ((), ())), preferred_element_type=F32)


def _dot_tn(a, b):
    return lax.dot_general(a, b, (((0,), (0,)), ((), ())), preferred_element_type=F32)


def _mod_kernel(c_ref, w_ref, b_ref, o_ref):
    a = _silu(c_ref[...]).astype(BF16)
    o_ref[0] = _dot(a, w_ref[0].astype(BF16)) + b_ref[0]


def _modulation(cond, w_mod, b_mod):
    depth, d, nd = w_mod.shape
    rows = cond.shape[0]
    bn = d
    return pl.pallas_call(
        _mod_kernel,
        out_shape=jax.ShapeDtypeStruct((depth, rows, nd), F32),
        grid=(depth, nd // bn),
        in_specs=[
            pl.BlockSpec((rows, d), lambda i, j: (0, 0)),
            pl.BlockSpec((1, d, bn), lambda i, j: (i, 0, j)),
            pl.BlockSpec((1, 1, bn), lambda i, j: (i, 0, j)),
        ],
        out_specs=pl.BlockSpec((1, rows, bn), lambda i, j: (i, 0, j)),
        compiler_params=_params(2),
        name="adaln_mod",
    )(cond, w_mod, b_mod.reshape(depth, 1, nd))


def _ffn_kernel(*refs, s, n_mix):
    x_ref, mod_ref, g_ref, wg_ref, wu_ref, wd_ref = refs[:6]
    mix_refs = refs[6:6 + n_mix]
    wo_ref = refs[6 + n_mix] if n_mix else None
    o_ref = refs[-1]
    x = x_ref[0]
    mod = mod_ref[0]
    if n_mix:
        y = None
        k0 = 0
        for m_ref in mix_refs:
            kk = m_ref.shape[-1]
            part = _dot(m_ref[0], wo_ref[k0:k0 + kk, :])
            y = part if y is None else y + part
            k0 += kk
        x = x + mod[5:6] * y
    xn = _rms_mod(x, g_ref[...], mod[3 * s:3 * s + 1], mod[3 * s + 1:3 * s + 2]).astype(BF16)
    gate = _dot(xn, wg_ref[...])
    up = _dot(xn, wu_ref[...])
    hid = (_silu(gate) * up).astype(BF16)
    o_ref[0] = x + (0.5 * mod[3 * s + 2:3 * s + 3]) * _dot(hid, wd_ref[...])


def _ffn(x, mod, g, wg, wu, wd, *, s, tm, mix=(), w_out=None):
    b, n, d = x.shape
    f = wg.shape[1]
    mod_map = (lambda i, j: (i, 0, 0)) if mod.shape[0] == b else (lambda i, j: (0, 0, 0))
    in_specs = [
        pl.BlockSpec((1, tm, d), lambda i, j: (i, j, 0)),
        pl.BlockSpec((1, N_MOD, d), mod_map),
        _resident((1, d)),
        _resident((d, f)),
        _resident((d, f)),
        _resident((f, d)),
    ]
    args = [x, mod, g, wg, wu, wd]
    for m in mix:
        in_specs.append(pl.BlockSpec((1, tm, m.shape[-1]), lambda i, j: (i, j, 0)))
        args.append(m)
    if mix:
        in_specs.append(_resident(w_out.shape))
        args.append(w_out)
    return pl.pallas_call(
        functools.partial(_ffn_kernel, s=s, n_mix=len(mix)),
        out_shape=jax.ShapeDtypeStruct(x.shape, F32),
        grid=(b, n // tm),
        in_specs=in_specs,
        out_specs=pl.BlockSpec((1, tm, d), lambda i, j: (i, j, 0)),
        compiler_params=_params(2),
        name="swiglu_half_step",
    )(*args)


def _halo_specs(n, tm, d):
    per = tm // HALO
    last = n // HALO - 1
    return [
        pl.BlockSpec((1, tm, d), lambda i, j: (i, j, 0)),
        pl.BlockSpec((1, HALO, d), lambda i, j: (i, jnp.maximum(j * per - 1, 0), 0)),
        pl.BlockSpec((1, HALO, d), lambda i, j: (i, jnp.minimum((j + 1) * per, last), 0)),
    ]


def _normed_ext_tile(x_ref, xp_ref, xn_ref, mod, g):
    xe = jnp.concatenate([xp_ref[0], x_ref[0], xn_ref[0]], axis=0)
    return _rms_mod(xe, g, mod[3:4], mod[4:5]).astype(BF16)


def _dead_halo_zeroed(pe, tm):
    j = pl.program_id(1)
    return jnp.concatenate([
        jnp.where(j == 0, 0.0, pe[:HALO]),
        pe[HALO:HALO + tm],
        jnp.where(j == pl.num_programs(1) - 1, 0.0, pe[HALO + tm:]),
    ], axis=0)


def _shift_rows(t, k):
    return pltpu.roll(t, k % t.shape[0], 0)


def _centred_window_sum(t, w):
    assert w & (w - 1) == 0 and w // 2 <= HALO
    span = 1
    while span < w:
        t = t + _shift_rows(t, span)
        span *= 2
    return _shift_rows(t, -(w - w // 2 - 1))


def _swap32(t):
    lane = lax.broadcasted_iota(jnp.int32, t.shape, 1)
    return jnp.where((lane % 64) < 32, pltpu.roll(t, HEAD_DIM - 32, 1), pltpu.roll(t, 32, 1))


def _ab_proj_kernel(*refs, tm, n_seq, rope):
    x_ref, xp_ref, xn_ref, mod_ref, g_ref, w_ref, qg_ref, kg_ref, pw_ref, ps_ref = refs[:10]
    if rope:
        cos_ref, sin_ref = refs[10:12]
    q_ref, k_ref, v_ref, y_ref = refs[-4:]
    mod = mod_ref[0]
    xe = _normed_ext_tile(x_ref, xp_ref, xn_ref, mod, g_ref[...])
    xm = xe[HALO:HALO + tm]
    nq = A_HEADS * HEAD_DIM
    nkv = A_KV_HEADS * HEAD_DIM

    def head_norm(t, gain):
        t = t * lax.rsqrt(jnp.mean(t * t, axis=-1, keepdims=True) + EPS) * gain
        if rope:
            t = t * cos_ref[...] + _swap32(t) * sin_ref[...]
        return t.astype(BF16)

    for blk in range((nq + nkv) // MXU_COLS):
        c0 = blk * MXU_COLS
        p = _dot(xm, w_ref[:, c0:c0 + MXU_COLS])
        out_ref, gain, col = (q_ref, qg_ref, c0) if c0 < nq else (k_ref, kg_ref, c0 - nq)
        for h in range(MXU_COLS // HEAD_DIM):
            lsl = slice(h * HEAD_DIM, (h + 1) * HEAD_DIM)
            out_ref[0, :, col + h * HEAD_DIM:col + (h + 1) * HEAD_DIM] = head_norm(p[:, lsl], gain[...])
    v_ref[0] = _dot(xm, w_ref[:, nq + nkv:nq + 2 * nkv]).astype(BF16)

    u0 = nq + 2 * nkv
    pos = pl.program_id(1) * tm + lax.broadcasted_iota(jnp.int32, (tm, 1), 0)
    per_blk = MXU_COLS // POOL_GROUP
    for blk in range(len(POOL_WINDOWS) // per_blk):
        ue = _dead_halo_zeroed(_dot(xe, w_ref[:, u0 + blk * MXU_COLS:u0 + (blk + 1) * MXU_COLS]), tm)
        for j in range(per_blk):
            gi = blk * per_blk + j
            w = POOL_WINDOWS[gi]
            sl = slice(gi * POOL_GROUP, (gi + 1) * POOL_GROUP)
            ug = ue[:, j * POOL_GROUP:(j + 1) * POOL_GROUP]
            tot = _centred_window_sum(ug, w)[HALO:HALO + tm]
            lo = jnp.maximum(pos - w // 2, 0)
            hi = jnp.minimum(pos + (w - w // 2), n_seq)
            cnt = (hi - lo).astype(F32)
            pooled = (tot / cnt - ug[HALO:HALO + tm]).astype(BF16)
            y_ref[0, :, sl] = (_dot(pooled, pw_ref[gi]) * ps_ref[:, sl]).astype(BF16)


def _ab_proj(x, mod, g, w_in, q_g, k_g, pool_w, pool_scale, cos_t, sin_t, *, tm):
    b, n, d = x.shape
    rope = cos_t is not None
    nq = A_HEADS * HEAD_DIM
    nkv = A_KV_HEADS * HEAD_DIM
    npool = POOL_GROUP * len(POOL_WINDOWS)
    mod_map = (lambda i, j: (i, 0, 0)) if mod.shape[0] == b else (lambda i, j: (0, 0, 0))
    in_specs = _halo_specs(n, tm, d) + [
        pl.BlockSpec((1, N_MOD, d), mod_map),
        _resident((1, d)),
        _resident(w_in.shape),
        _resident((1, HEAD_DIM)),
        _resident((1, HEAD_DIM)),
        _resident(pool_w.shape),
        _resident((1, npool)),
    ]
    args = [x, x, x, mod, g, w_in, q_g, k_g, pool_w, pool_scale]
    if rope:
        in_specs += [pl.BlockSpec((tm, HEAD_DIM), lambda i, j: (j, 0))] * 2
        args += [cos_t, sin_t]
    tile = lambda c: pl.BlockSpec((1, tm, c), lambda i, j: (i, j, 0))
    return pl.pallas_call(
        functools.partial(_ab_proj_kernel, tm=tm, n_seq=n, rope=rope),
        out_shape=(
            jax.ShapeDtypeStruct((b, n, nq), BF16),
            jax.ShapeDtypeStruct((b, n, nkv), BF16),
            jax.ShapeDtypeStruct((b, n, nkv), BF16),
            jax.ShapeDtypeStruct((b, n, npool), BF16),
        ),
        grid=(b, n // tm),
        in_specs=in_specs,
        out_specs=(tile(nq), tile(nkv), tile(nkv), tile(npool)),
        compiler_params=_params(2),
        name="attn_pool_proj",
    )(*args)


def _rope_tables(n):
    rows = n // GRID_W
    row = jnp.repeat(jnp.arange(rows), GRID_W).astype(F32)
    col = jnp.tile(jnp.arange(GRID_W), rows).astype(F32)
    half = HEAD_DIM // 2
    inv_freq = jnp.power(ROPE_THETA, -jnp.arange(0, half, 2, dtype=F32) / half)
    ang_r = row[:, None] * inv_freq
    ang_c = col[:, None] * inv_freq
    cos_t = jnp.concatenate([jnp.cos(ang_r)] * 2 + [jnp.cos(ang_c)] * 2, axis=-1)
    sin_t = jnp.concatenate([-jnp.sin(ang_r), jnp.sin(ang_r), -jnp.sin(ang_c), jnp.sin(ang_c)], axis=-1)
    return cos_t, sin_t


def _attn_kernel(*refs, n_kv_sets, tq):
    q_ref = refs[0]
    kv_refs = refs[1:1 + 2 * n_kv_sets]
    o_ref = refs[-1]
    scale = HEAD_DIM ** -0.5
    group = A_HEADS // A_KV_HEADS
    for g, r0 in [(g, r0) for r0 in range(0, q_ref.shape[1], tq) for g in range(A_KV_HEADS)]:
        ksl = slice(g * HEAD_DIM, (g + 1) * HEAD_DIM)
        qq = jnp.concatenate(
            [q_ref[0, r0:r0 + tq, (g * group + i) * HEAD_DIM:(g * group + i + 1) * HEAD_DIM] for i in range(group)],
            axis=0)
        scores = [_dot_nt(qq, kv_refs[2 * s][0, :, ksl]) for s in range(n_kv_sets)]
        m = None
        for sc in scores:
            ms = jnp.max(sc, axis=-1, keepdims=True)
            m = ms if m is None else jnp.maximum(m, ms)
        den = None
        acc = None
        for s, sc in enumerate(scores):
            e = jnp.exp2((sc - m) * (scale * math.log2(math.e)))
            ds = jnp.sum(e, axis=-1, keepdims=True)
            den = ds if den is None else den + ds
            pv = _dot(e.astype(BF16), kv_refs[2 * s + 1][0, :, ksl])
            acc = pv if acc is None else acc + pv
        out = acc / den
        for i in range(group):
            hq = g * group + i
            o_ref[0, r0:r0 + tq, hq * HEAD_DIM:(hq + 1) * HEAD_DIM] = out[i * tq:(i + 1) * tq].astype(BF16)


def _attention(q, kv_sets, *, tq, tb):
    b, n, nq = q.shape
    in_specs = [pl.BlockSpec((1, tb, nq), lambda i, j: (i, j, 0))]
    args = [q]
    for k, v in kv_sets:
        for t in (k, v):
            in_specs.append(pl.BlockSpec((1,) + t.shape[1:], lambda i, j: (i, 0, 0)))
            args.append(t)
    return pl.pallas_call(
        functools.partial(_attn_kernel, n_kv_sets=len(kv_sets), tq=tq),
        out_shape=jax.ShapeDtypeStruct(q.shape, BF16),
        grid=(b, n // tb),
        in_specs=in_specs,
        out_specs=pl.BlockSpec((1, tb, nq), lambda i, j: (i, j, 0)),
        compiler_params=_params(2),
        name="gqa_attention",
    )(*args)


def _segment_cumsum(y, seg, reverse):
    n = y.shape[0]
    r = lax.broadcasted_iota(jnp.int32, (n, 1), 0) % seg
    s = 1
    while s < seg:
        if reverse:
            y = y + jnp.where(r < seg - s, pltpu.roll(y, n - s, 0), 0.0)
        else:
            y = y + jnp.where(r >= s, pltpu.roll(y, s, 0), 0.0)
        s *= 2
    return y


def _gdn_proj_kernel(x_ref, xp_ref, xn_ref, mod_ref, g_ref, w_ref, cw_ref, al_ref, dt_ref,
                     q_ref, k_ref, v_ref, z_ref, gate_ref, *, tm):
    cdim = C_HEADS * HEAD_DIM
    mod = mod_ref[0]
    xe = _normed_ext_tile(x_ref, xp_ref, xn_ref, mod, g_ref[...])
    xm = xe[HALO:HALO + tm]

    for blk in range(3 * cdim // MXU_COLS):
        c0 = blk * MXU_COLS
        csl = slice(c0, c0 + MXU_COLS)
        pe = _dead_halo_zeroed(_dot(xe, w_ref[:, csl]), tm)
        conv = (_shift_rows(pe, 1)[HALO:HALO + tm] * cw_ref[0:1, csl]
                + pe[HALO:HALO + tm] * cw_ref[1:2, csl]
                + _shift_rows(pe, -1)[HALO:HALO + tm] * cw_ref[2:3, csl])
        act = _silu(conv)
        part, col = divmod(c0, cdim)
        out_ref = (q_ref, k_ref, v_ref)[part]
        for h in range(MXU_COLS // HEAD_DIM):
            t = act[:, h * HEAD_DIM:(h + 1) * HEAD_DIM]
            if part < 2:
                inv_norm = lax.rsqrt(jnp.sum(t * t, axis=-1, keepdims=True) + EPS)
                t = t * (inv_norm * (HEAD_DIM ** -0.5) if part == 0 else inv_norm)
            out_ref[0, :, col + h * HEAD_DIM:col + (h + 1) * HEAD_DIM] = t.astype(BF16)

    for blk in range(cdim // MXU_COLS):
        csl = slice(blk * MXU_COLS, (blk + 1) * MXU_COLS)
        z_ref[0, :, csl] = _dot(xm, w_ref[:, 3 * cdim + blk * MXU_COLS:3 * cdim + (blk + 1) * MXU_COLS]).astype(BF16)

    ab = _dot(xm, w_ref[:, 4 * cdim:])
    col = lax.broadcasted_iota(jnp.int32, ab.shape, 1)
    xa = ab + dt_ref[...]
    softplus = jnp.maximum(xa, 0.0) + jnp.log(1.0 + jnp.exp(-jnp.abs(xa)))
    gdec = -jnp.exp(al_ref[...]) * softplus
    is_a = (col % (2 * C_HEADS)) < C_HEADS
    gdec = jnp.where(is_a, gdec, 0.0)
    fwd = _segment_cumsum(gdec, CHUNK, reverse=False)
    bwd = _segment_cumsum(gdec, CHUNK, reverse=True)
    cum = jnp.where(col < 2 * C_HEADS, fwd, bwd)
    gate_ref[0] = jnp.where(is_a, cum, _sigmoid(ab))


def _gdn_proj(x, mod, g, w_in, conv_w, a_log_cols, dt_cols, *, tm):
    b, n, d = x.shape
    cdim = C_HEADS * HEAD_DIM
    ngate = 4 * C_HEADS
    mod_map = (lambda i, j: (i, 0, 0)) if mod.shape[0] == b else (lambda i, j: (0, 0, 0))
    in_specs = _halo_specs(n, tm, d) + [
        pl.BlockSpec((1, N_MOD, d), mod_map),
        _resident((1, d)),
        _resident(w_in.shape),
        _resident(conv_w.shape),
        _resident((1, ngate)),
        _resident((1, ngate)),
    ]
    tile = lambda c: pl.BlockSpec((1, tm, c), lambda i, j: (i, j, 0))
    return pl.pallas_call(
        functools.partial(_gdn_proj_kernel, tm=tm),
        out_shape=tuple(jax.ShapeDtypeStruct((b, n, cdim), BF16) for _ in range(4))
        + (jax.ShapeDtypeStruct((b, n, ngate), F32),),
        grid=(b, n // tm),
        in_specs=in_specs,
        out_specs=(tile(cdim),) * 4 + (tile(ngate),),
        compiler_params=_params(2),
        name="gdn_proj",
    )(x, x, x, mod, g, w_in, conv_w, a_log_cols, dt_cols)


GDN_HG = 2
GDN_CHAINS = 2 * GDN_HG
GDN_PACK = GDN_CHAINS * CHUNK
GDN_PREP_UNROLL = 8


def _chunk_rows(start):
    return pl.ds(start if isinstance(start, int) else pl.multiple_of(start, CHUNK), CHUNK)


def _bf16_mask(cond):
    return jnp.where(cond, 1.0, 0.0).astype(BF16)


def _gdn_prepare(q_ref, k_ref, v_ref, gc_ref, gr_ref, u_ref, w_ref, qd_ref, kt_ref, qk_ref, row0, cks):
    c = CHUNK
    ri = lax.broadcasted_iota(jnp.int32, (c, GDN_PACK), 0)
    li = lax.broadcasted_iota(jnp.int32, (c, GDN_PACK), 1)
    cj = li & (c - 1)
    blk = li >> 6
    rev = li >= GDN_HG * c
    hi_idx = jnp.where(rev, cj, ri)
    lo_idx = jnp.where(rev, ri, cj)
    incl = hi_idx >= lo_idx
    strict = hi_idx > lo_idx
    eye = jnp.where(ri == cj, 1.0, 0.0)
    bi = lax.broadcasted_iota(jnp.int32, (GDN_PACK, GDN_PACK), 0) >> 6
    bj = lax.broadcasted_iota(jnp.int32, (GDN_PACK, GDN_PACK), 1) >> 6
    bd_mask = _bf16_mask(bi == bj)
    chain_mask = [_bf16_mask(blk == a) for a in range(GDN_CHAINS)]
    head_lane = lax.broadcasted_iota(jnp.int32, (c, GDN_HG * HEAD_DIM), 1) >> 7

    def off_block(shift):
        return ((ri >> (shift + 1)) == (cj >> (shift + 1))) & ((ri >> shift) != (cj >> shift))

    def block_diag(p16):
        return jnp.concatenate([p16] * GDN_CHAINS, axis=0) * bd_mask

    def packed_col(cols, first):
        out = None
        for a in range(GDN_CHAINS):
            d, h = divmod(a, GDN_HG)
            j = d * 2 * GDN_HG + first + h
            col = jnp.broadcast_to(cols[:, j:j + 1], (c, GDN_PACK))
            out = col if out is None else jnp.where(blk == a, col, out)
        return out

    n = len(cks)
    rows = [_chunk_rows(ck * c) for ck in cks]
    srows = [_chunk_rows(row0 + ck * c) for ck in cks]
    k16 = [k_ref[0, r, :] for r in rows]
    q16 = [q_ref[0, r, :] for r in rows]
    v16 = [v_ref[0, r, :] for r in rows]
    gcol = [gc_ref[0, 0, r, :] for r in rows]
    grow = [gr_ref[0, 0, pl.ds(ck, 1), :] for ck in cks]

    decay = [jnp.exp(jnp.where(incl, packed_col(gcol[i], 0) - grow[i], NEG_BIG)) for i in range(n)]
    kdiag = [jnp.concatenate([jnp.where(head_lane == h, k16[i], jnp.zeros_like(k16[i])) for h in range(GDN_HG)], axis=0)
             for i in range(n)]
    gram = [_dot_nt(jnp.concatenate([k16[i], q16[i]], axis=0), kdiag[i]) for i in range(n)]
    kk = [jnp.concatenate([gram[i][:c]] * 2, axis=1) for i in range(n)]
    qk = [jnp.concatenate([gram[i][c:]] * 2, axis=1) for i in range(n)]
    low = [jnp.where(strict, packed_col(gcol[i], GDN_HG) * kk[i] * decay[i], 0.0) for i in range(n)]
    qkd = [jnp.where(incl, qk[i] * decay[i], 0.0).astype(BF16) for i in range(n)]
    for i in range(n):
        for a in range(GDN_CHAINS):
            qk_ref[a, srows[i], :] = qkd[i][:, a * c:(a + 1) * c]

    inv = [eye - jnp.where(off_block(0), low[i], 0.0) for i in range(n)]
    shift = 1
    while (1 << shift) < c:
        inv16 = [t.astype(BF16) for t in inv]
        mid = [block_diag(jnp.where(off_block(shift), low[i], 0.0).astype(BF16)) for i in range(n)]
        tmp = [_dot(inv16[i], mid[i]).astype(BF16) for i in range(n)]
        inv = [inv[i] - _dot(tmp[i], block_diag(inv16[i])) for i in range(n)]
        shift += 1
    inv16 = [t.astype(BF16) for t in inv]

    for i in range(n):
        rhs = []
        for a in range(GDN_CHAINS):
            d, h = divmod(a, GDN_HG)
            jg = d * 2 * GDN_HG + h
            hsl = slice(h * HEAD_DIM, (h + 1) * HEAD_DIM)
            gc_a = gcol[i][:, jg:jg + 1]
            beta_a = gcol[i][:, jg + GDN_HG:jg + GDN_HG + 1]
            egc = jnp.exp(gc_a)
            kf = k16[i][:, hsl].astype(F32)
            rhs.append(jnp.concatenate([v16[i][:, hsl].astype(F32) * beta_a, kf * (beta_a * egc)], axis=1).astype(BF16))
            last = gc_a[0:1] if d == 1 else gc_a[c - 1:c]
            qd_ref[a, srows[i], :] = (q16[i][:, hsl].astype(F32) * egc).astype(BF16)
            kt_ref[a, srows[i], :] = (kf * jnp.exp(last - gc_a)).astype(BF16)
        rhs = jnp.concatenate(rhs, axis=0)
        for a in range(GDN_CHAINS):
            sol = _dot(inv16[i] * chain_mask[a], rhs)
            u_ref[a, srows[i], :] = sol[:, :HEAD_DIM]
            w_ref[a, srows[i], :] = sol[:, HEAD_DIM:].astype(BF16)


def _gdn_scan_step(gc_ref, u_ref, w_ref, qd_ref, kt_ref, qk_ref, o_ref, st_ref, row0, nc, i, need_out):
    c = CHUNK
    cks = [i if a < GDN_HG else nc - 1 - i for a in range(GDN_CHAINS)]
    srows = [_chunk_rows(row0 + ck * c) for ck in cks]
    state = [st_ref[a] for a in range(GDN_CHAINS)]
    s16 = [s.astype(BF16) for s in state]
    if need_out:
        lhs = [jnp.concatenate([w_ref[a, srows[a], :], qd_ref[a, srows[a], :]], axis=0) for a in range(GDN_CHAINS)]
    else:
        lhs = [w_ref[a, srows[a], :] for a in range(GDN_CHAINS)]
    ws = [_dot(lhs[a], s16[a]) for a in range(GDN_CHAINS)]
    vn16 = [(u_ref[a, srows[a], :] - ws[a][:c]).astype(BF16) for a in range(GDN_CHAINS)]
    upd = [_dot_tn(kt_ref[a, srows[a], :], vn16[a]) for a in range(GDN_CHAINS)]
    for a in range(GDN_CHAINS):
        d, h = divmod(a, GDN_HG)
        jg = d * 2 * GDN_HG + h
        last_row = cks[a] * c + (0 if d == 1 else c - 1)
        g_tot = jnp.exp(gc_ref[0, 0, pl.ds(last_row, 1), :][:, jg:jg + 1])
        st_ref[a] = state[a] * g_tot + upd[a]
    if need_out:
        out = [ws[a][c:] + _dot(qk_ref[a, srows[a], :], vn16[a]) for a in range(GDN_CHAINS)]
        for a in range(GDN_CHAINS):
            d, h = divmod(a, GDN_HG)
            orow = _chunk_rows(cks[a] * c)
            o_ref[d, orow, h * HEAD_DIM:(h + 1) * HEAD_DIM] = out[a]


def _gdn_kernel(qx_ref, kx_ref, vx_ref, zx_ref, gxc_ref, gxr_ref, qh_ref, kh_ref, vh_ref, ghc_ref, ghr_ref,
                og_ref, y_ref, u_ref, w_ref, qd_ref, kt_ref, qk_ref, o_ref, st_ref):
    nx = qx_ref.shape[1]
    nh = qh_ref.shape[1]
    ncx = nx // CHUNK
    nch = nh // CHUNK
    scratch = (u_ref, w_ref, qd_ref, kt_ref, qk_ref)

    _gdn_prepare(qh_ref, kh_ref, vh_ref, ghc_ref, ghr_ref, *scratch, 0, list(range(nch)))

    def prep_body(it, carry):
        cks = [it * GDN_PREP_UNROLL + j for j in range(GDN_PREP_UNROLL)]
        _gdn_prepare(qx_ref, kx_ref, vx_ref, gxc_ref, gxr_ref, *scratch, nh, cks)
        return carry
    lax.fori_loop(0, ncx // GDN_PREP_UNROLL, prep_body, 0)

    st_ref[...] = jnp.zeros(st_ref.shape, F32)

    def ctx_body(i, carry):
        _gdn_scan_step(ghc_ref, *scratch, o_ref, st_ref, 0, nch, i, False)
        return carry
    lax.fori_loop(0, nch, ctx_body, 0)

    def x_body(i, carry):
        _gdn_scan_step(gxc_ref, *scratch, o_ref, st_ref, nh, ncx, i, True)
        return carry
    lax.fori_loop(0, ncx, x_body, 0)

    for h in range(GDN_HG):
        hsl = slice(h * HEAD_DIM, (h + 1) * HEAD_DIM)
        o = o_ref[0, :, hsl] + o_ref[1, :, hsl]
        y = o * lax.rsqrt(jnp.mean(o * o, axis=-1, keepdims=True) + EPS) * og_ref[...]
        y_ref[0, :, hsl] = (y * _silu(zx_ref[0, :, hsl].astype(F32))).astype(BF16)


def _gdn(qx, kx, vx, zx, gx, qh, kh, vh, gh, o_g):
    b, n, cdim = qx.shape
    nh = qh.shape[1]
    hg = GDN_HG
    ng = C_HEADS // hg
    ncol = 4 * hg
    assert n % (CHUNK * GDN_PREP_UNROLL) == 0 and nh % CHUNK == 0 and CHUNK == 64

    def regroup(gates, rows):
        t = gates.reshape(b, rows, 2, 2, ng, hg)
        cols = t.transpose(0, 4, 1, 2, 3, 5).reshape(b, ng, rows, ncol)
        gc = t[:, :, :, 0].reshape(b, rows // CHUNK, CHUNK, 2, ng, hg)
        return cols, gc.transpose(0, 4, 1, 3, 5, 2).reshape(b, ng, rows // CHUNK, GDN_PACK)

    gxc, gxr = regroup(gx, n)
    ghc, ghr = regroup(gh, nh)
    wdt = hg * HEAD_DIM
    tot = n + nh
    seq = lambda rows: pl.BlockSpec((1, rows, wdt), lambda i, j: (i, 0, j))
    gcol = lambda rows: pl.BlockSpec((1, 1, rows, ncol), lambda i, j: (i, j, 0, 0))
    grow = lambda rows: pl.BlockSpec((1, 1, rows // CHUNK, GDN_PACK), lambda i, j: (i, j, 0, 0))
    return pl.pallas_call(
        _gdn_kernel,
        out_shape=jax.ShapeDtypeStruct((b, n, cdim), BF16),
        grid=(b, ng),
        in_specs=[seq(n), seq(n), seq(n), seq(n), gcol(n), grow(n),
                  seq(nh), seq(nh), seq(nh), gcol(nh), grow(nh), _resident((1, HEAD_DIM))],
        out_specs=seq(n),
        scratch_shapes=[
            pltpu.VMEM((GDN_CHAINS, tot, HEAD_DIM), F32),
            pltpu.VMEM((GDN_CHAINS, tot, HEAD_DIM), BF16),
            pltpu.VMEM((GDN_CHAINS, tot, HEAD_DIM), BF16),
            pltpu.VMEM((GDN_CHAINS, tot, HEAD_DIM), BF16),
            pltpu.VMEM((GDN_CHAINS, tot, CHUNK), BF16),
            pltpu.VMEM((2, n, wdt), F32),
            pltpu.VMEM((GDN_CHAINS, HEAD_DIM, HEAD_DIM), F32),
        ],
        compiler_params=_params(2),
        name="gated_delta",
    )(qx, kx, vx, zx, gxc, gxr, qh, kh, vh, ghc, ghr, o_g)


def _gate_columns(vals):
    zeros = jnp.zeros_like(vals[0])
    return jnp.concatenate([vals[0], zeros, vals[1], zeros])[None].astype(F32)


def kernel(x, c, ctx, c_ctx, w_mod, b_mod, norm_g, ffn_wg, ffn_wu, ffn_wd, ab_w_in, ab_q_norm, ab_k_norm,
           pool_w, pool_scale, ab_w_out, gdn_w_in, gdn_conv_w, gdn_a_log, gdn_dt_bias, gdn_o_norm, gdn_w_out):
    depth = w_mod.shape[0]
    b, n, d = x.shape
    n_ctx = ctx.shape[1]
    tm_x = min(512, n)
    tm_h = n_ctx

    pad = (-(b + 1)) % SUBLANES
    cond = jnp.concatenate([c, c_ctx[None], jnp.zeros((pad, d), F32)], axis=0)
    mods = _modulation(cond, w_mod, b_mod)
    cos_t, sin_t = _rope_tables(n)

    h = ctx
    for i in range(depth):
        last = i == depth - 1
        mx = mods[i, :b].reshape(b, N_MOD, d)
        mh = mods[i, b:b + 1].reshape(1, N_MOD, d)
        wg, wu, wd = (t[i].astype(BF16) for t in (ffn_wg, ffn_wu, ffn_wd))
        g0, g1, g2 = (norm_g[i, s][None] for s in range(3))
        x = _ffn(x, mx, g0, wg[0], wu[0], wd[0], s=0, tm=tm_x)
        h = _ffn(h, mh, g0, wg[0], wu[0], wd[0], s=0, tm=tm_h)
        j = i // 2
        if i % 2 == 0:
            w_in = ab_w_in[j].astype(BF16)
            pw = pool_w[j].astype(BF16)
            common = (w_in, ab_q_norm[j][None], ab_k_norm[j][None], pw, pool_scale[j][None])
            qx, kx, vx, px = _ab_proj(x, mx, g1, *common, cos_t, sin_t, tm=tm_x)
            qh, kh, vh, ph = _ab_proj(h, mh, g1, *common, None, None, tm=tm_h)
            ax = _attention(qx, [(kh, vh), (kx, vx)], tq=min(256, n), tb=min(512, n))
            mix_x = (ax, px)
            w_out = ab_w_out[j].astype(BF16)
            if not last:
                ah = _attention(qh, [(kh, vh)], tq=min(256, n_ctx), tb=min(256, n_ctx))
                mix_h = (ah, ph)
        else:
            w_in = gdn_w_in[j].astype(BF16)
            al = _gate_columns(gdn_a_log[j])
            dt = _gate_columns(gdn_dt_bias[j])
            qx, kx, vx, zx, gx = _gdn_proj(x, mx, g1, w_in, gdn_conv_w[j], al, dt, tm=tm_x)
            qh, kh, vh, zh, gh = _gdn_proj(h, mh, g1, w_in, gdn_conv_w[j], al, dt, tm=tm_h)
            if not last:
                raise NotImplementedError("context output of the DeltaNet mixer is only needed for depth > 2")
            mix_x = (_gdn(qx, kx, vx, zx, gx, qh, kh, vh, gh, gdn_o_norm[j][None]),)
            w_out = gdn_w_out[j].astype(BF16)
        x = _ffn(x, mx, g2, wg[1], wu[1], wd[1], s=2, tm=tm_x, mix=mix_x, w_out=w_out)
        if not last:
            h = _ffn(h, mh, g2, wg[1], wu[1], wd[1], s=2, tm=tm_h, mix=mix_h, w_out=w_out)
    return x
```

```python
import functools
import math

import jax
import jax.numpy as jnp
from jax import lax
from jax.experimental import pallas as pl
from jax.experimental.pallas import tpu as pltpu

F32 = jnp.float32
BF16 = jnp.bfloat16

EPS = 1e-6
HEAD_DIM = 128
GRID_W = 64
ROPE_THETA = 10000.0
N_MOD = 9
A_HEADS = 4
A_KV_HEADS = 2
POOL_WINDOWS = (2, 4, 8, 16)
POOL_GROUP = 128
C_HEADS = 8
CHUNK = 64

SUBLANES = 8
MXU_COLS = 256
VMEM_LIMIT_BYTES = 56 * 1024 * 1024
HALO = SUBLANES

NEG_BIG = -1e30


def _params(n_grid, vmem=VMEM_LIMIT_BYTES):
    return pltpu.CompilerParams(dimension_semantics=("arbitrary",) * n_grid, vmem_limit_bytes=vmem)


def _resident(shape):
    nd = len(shape)
    return pl.BlockSpec(shape, lambda *_: (0,) * nd, pipeline_mode=pl.Buffered(1))


def _sigmoid(t):
    return 1.0 / (1.0 + jnp.exp(-t))


def _silu(t):
    h = 0.5 * t
    return h + h * jnp.tanh(h)


def _rms_mod(x, g, shift, scale):
    y = x * lax.rsqrt(jnp.mean(x * x, axis=-1, keepdims=True) + EPS)
    return (y * g) * (1.0 + scale) + shift


def _dot(a, b):
    return jnp.dot(a, b, preferred_element_type=F32)


def _dot_nt(a, b):
    return lax.dot_general(a, b, (((1,), (1,)), ((), ())), preferred_element_type=F32)


def _dot_tn(a, b):
    return lax.dot_general(a, b, (((0,), (0,)), ((), ())), preferred_element_type=F32)


def _mod_kernel(c_ref, w_ref, b_ref, o_ref):
    a = _silu(c_ref[...]).astype(BF16)
    o_ref[0] = _dot(a, w_ref[0].astype(BF16)) + b_ref[0]


def _modulation(cond, w_mod, b_mod):
    depth, d, nd = w_mod.shape
    rows = cond.shape[0]
    bn = d
    return pl.pallas_call(
        _mod_kernel,
        out_shape=jax.ShapeDtypeStruct((depth, rows, nd), F32),
        grid=(depth, nd // bn),
        in_specs=[
            pl.BlockSpec((rows, d), lambda i, j: (0, 0)),
            pl.BlockSpec((1, d, bn), lambda i, j: (i, 0, j)),
            pl.BlockSpec((1, 1, bn), lambda i, j: (i, 0, j)),
        ],
        out_specs=pl.BlockSpec((1, rows, bn), lambda i, j: (i, 0, j)),
        compiler_params=_params(2),
        name="adaln_mod",
    )(cond, w_mod, b_mod.reshape(depth, 1, nd))


def _ffn_kernel(*refs, s, n_mix):
    x_ref, mod_ref, g_ref, wg_ref, wu_ref, wd_ref = refs[:6]
    mix_refs = refs[6:6 + n_mix]
    wo_ref = refs[6 + n_mix] if n_mix else None
    o_ref = refs[-1]
    x = x_ref[0]
    mod = mod_ref[0]
    if n_mix:
        y = None
        k0 = 0
        for m_ref in mix_refs:
            kk = m_ref.shape[-1]
            part = _dot(m_ref[0], wo_ref[k0:k0 + kk, :])
            y = part if y is None else y + part
            k0 += kk
        x = x + mod[5:6] * y
    xn = _rms_mod(x, g_ref[...], mod[3 * s:3 * s + 1], mod[3 * s + 1:3 * s + 2]).astype(BF16)
    gate = _dot(xn, wg_ref[...])
    up = _dot(xn, wu_ref[...])
    hid = (_silu(gate) * up).astype(BF16)
    o_ref[0] = x + (0.5 * mod[3 * s + 2:3 * s + 3]) * _dot(hid, wd_ref[...])


def _ffn(x, mod, g, wg, wu, wd, *, s, tm, mix=(), w_out=None):
    b, n, d = x.shape
    f = wg.shape[1]
    mod_map = (lambda i, j: (i, 0, 0)) if mod.shape[0] == b else (lambda i, j: (0, 0, 0))
    in_specs = [
        pl.BlockSpec((1, tm, d), lambda i, j: (i, j, 0)),
        pl.BlockSpec((1, N_MOD, d), mod_map),
        _resident((1, d)),
        _resident((d, f)),
        _resident((d, f)),
        _resident((f, d)),
    ]
    args = [x, mod, g, wg, wu, wd]
    for m in mix:
        in_specs.append(pl.BlockSpec((1, tm, m.shape[-1]), lambda i, j: (i, j, 0)))
        args.append(m)
    if mix:
        in_specs.append(_resident(w_out.shape))
        args.append(w_out)
    return pl.pallas_call(
        functools.partial(_ffn_kernel, s=s, n_mix=len(mix)),
        out_shape=jax.ShapeDtypeStruct(x.shape, F32),
        grid=(b, n // tm),
        in_specs=in_specs,
        out_specs=pl.BlockSpec((1, tm, d), lambda i, j: (i, j, 0)),
        compiler_params=_params(2),
        name="swiglu_half_step",
    )(*args)


def _halo_specs(n, tm, d):
    per = tm // HALO
    last = n // HALO - 1
    return [
        pl.BlockSpec((1, tm, d), lambda i, j: (i, j, 0)),
        pl.BlockSpec((1, HALO, d), lambda i, j: (i, jnp.maximum(j * per - 1, 0), 0)),
        pl.BlockSpec((1, HALO, d), lambda i, j: (i, jnp.minimum((j + 1) * per, last), 0)),
    ]


def _normed_ext_tile(x_ref, xp_ref, xn_ref, mod, g):
    xe = jnp.concatenate([xp_ref[0], x_ref[0], xn_ref[0]], axis=0)
    return _rms_mod(xe, g, mod[3:4], mod[4:5]).astype(BF16)


def _dead_halo_zeroed(pe, tm):
    j = pl.program_id(1)
    return jnp.concatenate([
        jnp.where(j == 0, 0.0, pe[:HALO]),
        pe[HALO:HALO + tm],
        jnp.where(j == pl.num_programs(1) - 1, 0.0, pe[HALO + tm:]),
    ], axis=0)


def _shift_rows(t, k):
    return pltpu.roll(t, k % t.shape[0], 0)


def _centred_window_sum(t, w):
    assert w & (w - 1) == 0 and w // 2 <= HALO
    span = 1
    while span < w:
        t = t + _shift_rows(t, span)
        span *= 2
    return _shift_rows(t, -(w - w // 2 - 1))


def _swap32(t):
    half, quarter = HEAD_DIM // 2, HEAD_DIM // 4
    lane = lax.broadcasted_iota(jnp.int32, t.shape, 1)
    return jnp.where((lane % half) < quarter, pltpu.roll(t, HEAD_DIM - quarter, 1), pltpu.roll(t, quarter, 1))


def _ab_proj_kernel(*refs, tm, n_seq, rope):
    x_ref, xp_ref, xn_ref, mod_ref, g_ref, w_ref, qg_ref, kg_ref, pw_ref, ps_ref = refs[:10]
    if rope:
        cos_ref, sin_ref = refs[10:12]
    q_ref, k_ref, v_ref, y_ref = refs[-4:]
    mod = mod_ref[0]
    xe = _normed_ext_tile(x_ref, xp_ref, xn_ref, mod, g_ref[...])
    xm = xe[HALO:HALO + tm]
    nq = A_HEADS * HEAD_DIM
    nkv = A_KV_HEADS * HEAD_DIM

    def head_norm(t, gain):
        t = t * lax.rsqrt(jnp.mean(t * t, axis=-1, keepdims=True) + EPS) * gain
        if rope:
            t = t * cos_ref[...] + _swap32(t) * sin_ref[...]
        return t.astype(BF16)

    for blk in range((nq + nkv) // MXU_COLS):
        c0 = blk * MXU_COLS
        p = _dot(xm, w_ref[:, c0:c0 + MXU_COLS])
        out_ref, gain, col = (q_ref, qg_ref, c0) if c0 < nq else (k_ref, kg_ref, c0 - nq)
        for h in range(MXU_COLS // HEAD_DIM):
            lsl = slice(h * HEAD_DIM, (h + 1) * HEAD_DIM)
            out_ref[0, :, col + h * HEAD_DIM:col + (h + 1) * HEAD_DIM] = head_norm(p[:, lsl], gain[...])
    v_ref[0] = _dot(xm, w_ref[:, nq + nkv:nq + 2 * nkv]).astype(BF16)

    u0 = nq + 2 * nkv
    pos = pl.program_id(1) * tm + lax.broadcasted_iota(jnp.int32, (tm, 1), 0)
    per_blk = MXU_COLS // POOL_GROUP
    for blk in range(len(POOL_WINDOWS) // per_blk):
        ue = _dead_halo_zeroed(_dot(xe, w_ref[:, u0 + blk * MXU_COLS:u0 + (blk + 1) * MXU_COLS]), tm)
        for j in range(per_blk):
            gi = blk * per_blk + j
            w = POOL_WINDOWS[gi]
            sl = slice(gi * POOL_GROUP, (gi + 1) * POOL_GROUP)
            ug = ue[:, j * POOL_GROUP:(j + 1) * POOL_GROUP]
            tot = _centred_window_sum(ug, w)[HALO:HALO + tm]
            lo = jnp.maximum(pos - w // 2, 0)
            hi = jnp.minimum(pos + (w - w // 2), n_seq)
            cnt = (hi - lo).astype(F32)
            pooled = (tot / cnt - ug[HALO:HALO + tm]).astype(BF16)
            y_ref[0, :, sl] = (_dot(pooled, pw_ref[gi]) * ps_ref[:, sl]).astype(BF16)


def _ab_proj(x, mod, g, w_in, q_g, k_g, pool_w, pool_scale, cos_t, sin_t, *, tm):
    b, n, d = x.shape
    rope = cos_t is not None
    nq = A_HEADS * HEAD_DIM
    nkv = A_KV_HEADS * HEAD_DIM
    npool = POOL_GROUP * len(POOL_WINDOWS)
    mod_map = (lambda i, j: (i, 0, 0)) if mod.shape[0] == b else (lambda i, j: (0, 0, 0))
    in_specs = _halo_specs(n, tm, d) + [
        pl.BlockSpec((1, N_MOD, d), mod_map),
        _resident((1, d)),
        _resident(w_in.shape),
        _resident((1, HEAD_DIM)),
        _resident((1, HEAD_DIM)),
        _resident(pool_w.shape),
        _resident((1, npool)),
    ]
    args = [x, x, x, mod, g, w_in, q_g, k_g, pool_w, pool_scale]
    if rope:
        in_specs += [pl.BlockSpec((tm, HEAD_DIM), lambda i, j: (j, 0))] * 2
        args += [cos_t, sin_t]
    tile = lambda c: pl.BlockSpec((1, tm, c), lambda i, j: (i, j, 0))
    return pl.pallas_call(
        functools.partial(_ab_proj_kernel, tm=tm, n_seq=n, rope=rope),
        out_shape=(
            jax.ShapeDtypeStruct((b, n, nq), BF16),
            jax.ShapeDtypeStruct((b, n, nkv), BF16),
            jax.ShapeDtypeStruct((b, n, nkv), BF16),
            jax.ShapeDtypeStruct((b, n, npool), BF16),
        ),
        grid=(b, n // tm),
        in_specs=in_specs,
        out_specs=(tile(nq), tile(nkv), tile(nkv), tile(npool)),
        compiler_params=_params(2),
        name="attn_pool_proj",
    )(*args)


def _rope_tables(n):
    rows = n // GRID_W
    row = jnp.repeat(jnp.arange(rows), GRID_W).astype(F32)
    col = jnp.tile(jnp.arange(GRID_W), rows).astype(F32)
    half = HEAD_DIM // 2
    inv_freq = jnp.power(ROPE_THETA, -jnp.arange(0, half, 2, dtype=F32) / half)
    ang_r = row[:, None] * inv_freq
    ang_c = col[:, None] * inv_freq
    cos_t = jnp.concatenate([jnp.cos(ang_r)] * 2 + [jnp.cos(ang_c)] * 2, axis=-1)
    sin_t = jnp.concatenate([-jnp.sin(ang_r), jnp.sin(ang_r), -jnp.sin(ang_c), jnp.sin(ang_c)], axis=-1)
    return cos_t, sin_t


def _attn_kernel(*refs, n_kv_sets, tq):
    q_ref = refs[0]
    kv_refs = refs[1:1 + 2 * n_kv_sets]
    o_ref = refs[-1]
    scale = HEAD_DIM ** -0.5
    group = A_HEADS // A_KV_HEADS
    for g, r0 in [(g, r0) for r0 in range(0, q_ref.shape[1], tq) for g in range(A_KV_HEADS)]:
        ksl = slice(g * HEAD_DIM, (g + 1) * HEAD_DIM)
        qq = jnp.concatenate(
            [q_ref[0, r0:r0 + tq, (g * group + i) * HEAD_DIM:(g * group + i + 1) * HEAD_DIM] for i in range(group)],
            axis=0)
        scores = [_dot_nt(qq, kv_refs[2 * s][0, :, ksl]) for s in range(n_kv_sets)]
        m = None
        for sc in scores:
            ms = jnp.max(sc, axis=-1, keepdims=True)
            m = ms if m is None else jnp.maximum(m, ms)
        den = None
        acc = None
        for s, sc in enumerate(scores):
            e = jnp.exp2((sc - m) * (scale * math.log2(math.e)))
            ds = jnp.sum(e, axis=-1, keepdims=True)
            den = ds if den is None else den + ds
            pv = _dot(e.astype(BF16), kv_refs[2 * s + 1][0, :, ksl])
            acc = pv if acc is None else acc + pv
        out = acc / den
        for i in range(group):
            hq = g * group + i
            o_ref[0, r0:r0 + tq, hq * HEAD_DIM:(hq + 1) * HEAD_DIM] = out[i * tq:(i + 1) * tq].astype(BF16)


def _attention(q, kv_sets, *, tq, tb):
    b, n, nq = q.shape
    in_specs = [pl.BlockSpec((1, tb, nq), lambda i, j: (i, j, 0))]
    args = [q]
    for k, v in kv_sets:
        for t in (k, v):
            in_specs.append(pl.BlockSpec((1,) + t.shape[1:], lambda i, j: (i, 0, 0)))
            args.append(t)
    return pl.pallas_call(
        functools.partial(_attn_kernel, n_kv_sets=len(kv_sets), tq=tq),
        out_shape=jax.ShapeDtypeStruct(q.shape, BF16),
        grid=(b, n // tb),
        in_specs=in_specs,
        out_specs=pl.BlockSpec((1, tb, nq), lambda i, j: (i, j, 0)),
        compiler_params=_params(2),
        name="gqa_attention",
    )(*args)


def _segment_cumsum(y, seg, reverse):
    n = y.shape[0]
    r = lax.broadcasted_iota(jnp.int32, (n, 1), 0) % seg
    s = 1
    while s < seg:
        if reverse:
            y = y + jnp.where(r < seg - s, pltpu.roll(y, n - s, 0), 0.0)
        else:
            y = y + jnp.where(r >= s, pltpu.roll(y, s, 0), 0.0)
        s *= 2
    return y


def _gdn_proj_kernel(x_ref, xp_ref, xn_ref, mod_ref, g_ref, w_ref, cw_ref, al_ref, dt_ref,
                     q_ref, k_ref, v_ref, z_ref, gate_ref, *, tm):
    cdim = C_HEADS * HEAD_DIM
    mod = mod_ref[0]
    xe = _normed_ext_tile(x_ref, xp_ref, xn_ref, mod, g_ref[...])
    xm = xe[HALO:HALO + tm]

    for blk in range(3 * cdim // MXU_COLS):
        c0 = blk * MXU_COLS
        csl = slice(c0, c0 + MXU_COLS)
        pe = _dead_halo_zeroed(_dot(xe, w_ref[:, csl]), tm)
        conv = (_shift_rows(pe, 1)[HALO:HALO + tm] * cw_ref[0:1, csl]
                + pe[HALO:HALO + tm] * cw_ref[1:2, csl]
                + _shift_rows(pe, -1)[HALO:HALO + tm] * cw_ref[2:3, csl])
        act = _silu(conv)
        part, col = divmod(c0, cdim)
        out_ref = (q_ref, k_ref, v_ref)[part]
        for h in range(MXU_COLS // HEAD_DIM):
            t = act[:, h * HEAD_DIM:(h + 1) * HEAD_DIM]
            if part < 2:
                inv_norm = lax.rsqrt(jnp.sum(t * t, axis=-1, keepdims=True) + EPS)
                t = t * (inv_norm * (HEAD_DIM ** -0.5) if part == 0 else inv_norm)
            out_ref[0, :, col + h * HEAD_DIM:col + (h + 1) * HEAD_DIM] = t.astype(BF16)

    for blk in range(cdim // MXU_COLS):
        csl = slice(blk * MXU_COLS, (blk + 1) * MXU_COLS)
        z_ref[0, :, csl] = _dot(xm, w_ref[:, 3 * cdim + blk * MXU_COLS:3 * cdim + (blk + 1) * MXU_COLS]).astype(BF16)

    ab = _dot(xm, w_ref[:, 4 * cdim:])
    col = lax.broadcasted_iota(jnp.int32, ab.shape, 1)
    xa = ab + dt_ref[...]
    softplus = jnp.maximum(xa, 0.0) + jnp.log(1.0 + jnp.exp(-jnp.abs(xa)))
    gdec = -jnp.exp(al_ref[...]) * softplus
    is_a = (col % (2 * C_HEADS)) < C_HEADS
    gdec = jnp.where(is_a, gdec, 0.0)
    fwd = _segment_cumsum(gdec, CHUNK, reverse=False)
    bwd = _segment_cumsum(gdec, CHUNK, reverse=True)
    cum = jnp.where(col < 2 * C_HEADS, fwd, bwd)
    gate_ref[0] = jnp.where(is_a, cum, _sigmoid(ab))


def _gdn_proj(x, mod, g, w_in, conv_w, a_log_cols, dt_cols, *, tm):
    b, n, d = x.shape
    cdim = C_HEADS * HEAD_DIM
    ngate = 4 * C_HEADS
    mod_map = (lambda i, j: (i, 0, 0)) if mod.shape[0] == b else (lambda i, j: (0, 0, 0))
    in_specs = _halo_specs(n, tm, d) + [
        pl.BlockSpec((1, N_MOD, d), mod_map),
        _resident((1, d)),
        _resident(w_in.shape),
        _resident(conv_w.shape),
        _resident((1, ngate)),
        _resident((1, ngate)),
    ]
    tile = lambda c: pl.BlockSpec((1, tm, c), lambda i, j: (i, j, 0))
    return pl.pallas_call(
        functools.partial(_gdn_proj_kernel, tm=tm),
        out_shape=tuple(jax.ShapeDtypeStruct((b, n, cdim), BF16) for _ in range(4))
        + (jax.ShapeDtypeStruct((b, n, ngate), F32),),
        grid=(b, n // tm),
        in_specs=in_specs,
        out_specs=(tile(cdim),) * 4 + (tile(ngate),),
        compiler_params=_params(2),
        name="gdn_proj",
    )(x, x, x, mod, g, w_in, conv_w, a_log_cols, dt_cols)


GDN_HG = 2
GDN_CHAINS = 2 * GDN_HG
GDN_PACK = GDN_CHAINS * CHUNK
GDN_PREP_UNROLL = 8


def _chunk_rows(start):
    return pl.ds(start if isinstance(start, int) else pl.multiple_of(start, CHUNK), CHUNK)


def _bf16_mask(cond):
    return jnp.where(cond, 1.0, 0.0).astype(BF16)


def _gdn_prepare(q_ref, k_ref, v_ref, gc_ref, gr_ref, u_ref, w_ref, qd_ref, kt_ref, qk_ref, row0, cks):
    c = CHUNK
    ri = lax.broadcasted_iota(jnp.int32, (c, GDN_PACK), 0)
    li = lax.broadcasted_iota(jnp.int32, (c, GDN_PACK), 1)
    cj = li % c
    blk = li // c
    rev = li >= GDN_HG * c
    hi_idx = jnp.where(rev, cj, ri)
    lo_idx = jnp.where(rev, ri, cj)
    incl = hi_idx >= lo_idx
    strict = hi_idx > lo_idx
    eye = jnp.where(ri == cj, 1.0, 0.0)
    bi = lax.broadcasted_iota(jnp.int32, (GDN_PACK, GDN_PACK), 0) // c
    bj = lax.broadcasted_iota(jnp.int32, (GDN_PACK, GDN_PACK), 1) // c
    bd_mask = _bf16_mask(bi == bj)
    chain_mask = [_bf16_mask(blk == a) for a in range(GDN_CHAINS)]
    head_lane = lax.broadcasted_iota(jnp.int32, (c, GDN_HG * HEAD_DIM), 1) // HEAD_DIM

    def off_block(shift):
        return ((ri >> (shift + 1)) == (cj >> (shift + 1))) & ((ri >> shift) != (cj >> shift))

    def block_diag(p16):
        return jnp.concatenate([p16] * GDN_CHAINS, axis=0) * bd_mask

    def packed_col(cols, first):
        out = None
        for a in range(GDN_CHAINS):
            d, h = divmod(a, GDN_HG)
            j = d * 2 * GDN_HG + first + h
            col = jnp.broadcast_to(cols[:, j:j + 1], (c, GDN_PACK))
            out = col if out is None else jnp.where(blk == a, col, out)
        return out

    n = len(cks)
    rows = [_chunk_rows(ck * c) for ck in cks]
    srows = [_chunk_rows(row0 + ck * c) for ck in cks]
    k16 = [k_ref[0, r, :] for r in rows]
    q16 = [q_ref[0, r, :] for r in rows]
    v16 = [v_ref[0, r, :] for r in rows]
    gcol = [gc_ref[0, 0, r, :] for r in rows]
    grow = [gr_ref[0, 0, pl.ds(ck, 1), :] for ck in cks]

    decay = [jnp.exp(jnp.where(incl, packed_col(gcol[i], 0) - grow[i], NEG_BIG)) for i in range(n)]
    kdiag = [jnp.concatenate([jnp.where(head_lane == h, k16[i], jnp.zeros_like(k16[i])) for h in range(GDN_HG)], axis=0)
             for i in range(n)]
    gram = [_dot_nt(jnp.concatenate([k16[i], q16[i]], axis=0), kdiag[i]) for i in range(n)]
    kk = [jnp.concatenate([gram[i][:c]] * 2, axis=1) for i in range(n)]
    qk = [jnp.concatenate([gram[i][c:]] * 2, axis=1) for i in range(n)]
    low = [jnp.where(strict, packed_col(gcol[i], GDN_HG) * kk[i] * decay[i], 0.0) for i in range(n)]
    qkd = [jnp.where(incl, qk[i] * decay[i], 0.0).astype(BF16) for i in range(n)]
    for i in range(n):
        for a in range(GDN_CHAINS):
            qk_ref[a, srows[i], :] = qkd[i][:, a * c:(a + 1) * c]

    inv = [eye - jnp.where(off_block(0), low[i], 0.0) for i in range(n)]
    shift = 1
    while (1 << shift) < c:
        inv16 = [t.astype(BF16) for t in inv]
        mid = [block_diag(jnp.where(off_block(shift), low[i], 0.0).astype(BF16)) for i in range(n)]
        tmp = [_dot(inv16[i], mid[i]).astype(BF16) for i in range(n)]
        inv = [inv[i] - _dot(tmp[i], block_diag(inv16[i])) for i in range(n)]
        shift += 1
    inv16 = [t.astype(BF16) for t in inv]

    for i in range(n):
        rhs = []
        for a in range(GDN_CHAINS):
            d, h = divmod(a, GDN_HG)
            jg = d * 2 * GDN_HG + h
            hsl = slice(h * HEAD_DIM, (h + 1) * HEAD_DIM)
            gc_a = gcol[i][:, jg:jg + 1]
            beta_a = gcol[i][:, jg + GDN_HG:jg + GDN_HG + 1]
            egc = jnp.exp(gc_a)
            kf = k16[i][:, hsl].astype(F32)
            rhs.append(jnp.concatenate([v16[i][:, hsl].astype(F32) * beta_a, kf * (beta_a * egc)], axis=1).astype(BF16))
            last = gc_a[0:1] if d == 1 else gc_a[c - 1:c]
            qd_ref[a, srows[i], :] = (q16[i][:, hsl].astype(F32) * egc).astype(BF16)
            kt_ref[a, srows[i], :] = (kf * jnp.exp(last - gc_a)).astype(BF16)
        rhs = jnp.concatenate(rhs, axis=0)
        for a in range(GDN_CHAINS):
            sol = _dot(inv16[i] * chain_mask[a], rhs)
            u_ref[a, srows[i], :] = sol[:, :HEAD_DIM]
            w_ref[a, srows[i], :] = sol[:, HEAD_DIM:].astype(BF16)


def _gdn_scan_step(gc_ref, u_ref, w_ref, qd_ref, kt_ref, qk_ref, o_ref, st_ref, row0, nc, i, need_out):
    c = CHUNK
    cks = [i if a < GDN_HG else nc - 1 - i for a in range(GDN_CHAINS)]
    srows = [_chunk_rows(row0 + ck * c) for ck in cks]
    state = [st_ref[a] for a in range(GDN_CHAINS)]
    s16 = [s.astype(BF16) for s in state]
    if need_out:
        lhs = [jnp.concatenate([w_ref[a, srows[a], :], qd_ref[a, srows[a], :]], axis=0) for a in range(GDN_CHAINS)]
    else:
        lhs = [w_ref[a, srows[a], :] for a in range(GDN_CHAINS)]
    ws = [_dot(lhs[a], s16[a]) for a in range(GDN_CHAINS)]
    vn16 = [(u_ref[a, srows[a], :] - ws[a][:c]).astype(BF16) for a in range(GDN_CHAINS)]
    upd = [_dot_tn(kt_ref[a, srows[a], :], vn16[a]) for a in range(GDN_CHAINS)]
    for a in range(GDN_CHAINS):
        d, h = divmod(a, GDN_HG)
        jg = d * 2 * GDN_HG + h
        last_row = cks[a] * c + (0 if d == 1 else c - 1)
        g_tot = jnp.exp(gc_ref[0, 0, pl.ds(last_row, 1), :][:, jg:jg + 1])
        st_ref[a] = state[a] * g_tot + upd[a]
    if need_out:
        out = [ws[a][c:] + _dot(qk_ref[a, srows[a], :], vn16[a]) for a in range(GDN_CHAINS)]
        for a in range(GDN_CHAINS):
            d, h = divmod(a, GDN_HG)
            orow = _chunk_rows(cks[a] * c)
            o_ref[d, orow, h * HEAD_DIM:(h + 1) * HEAD_DIM] = out[a]


def _gdn_kernel(qx_ref, kx_ref, vx_ref, zx_ref, gxc_ref, gxr_ref, qh_ref, kh_ref, vh_ref, ghc_ref, ghr_ref,
                og_ref, y_ref, u_ref, w_ref, qd_ref, kt_ref, qk_ref, o_ref, st_ref):
    nx = qx_ref.shape[1]
    nh = qh_ref.shape[1]
    ncx = nx // CHUNK
    nch = nh // CHUNK
    scratch = (u_ref, w_ref, qd_ref, kt_ref, qk_ref)

    _gdn_prepare(qh_ref, kh_ref, vh_ref, ghc_ref, ghr_ref, *scratch, 0, list(range(nch)))

    def prep_body(it, carry):
        cks = [it * GDN_PREP_UNROLL + j for j in range(GDN_PREP_UNROLL)]
        _gdn_prepare(qx_ref, kx_ref, vx_ref, gxc_ref, gxr_ref, *scratch, nh, cks)
        return carry
    lax.fori_loop(0, ncx // GDN_PREP_UNROLL, prep_body, 0)

    st_ref[...] = jnp.zeros(st_ref.shape, F32)

    def ctx_body(i, carry):
        _gdn_scan_step(ghc_ref, *scratch, o_ref, st_ref, 0, nch, i, False)
        return carry
    lax.fori_loop(0, nch, ctx_body, 0)

    def x_body(i, carry):
        _gdn_scan_step(gxc_ref, *scratch, o_ref, st_ref, nh, ncx, i, True)
        return carry
    lax.fori_loop(0, ncx, x_body, 0)

    for h in range(GDN_HG):
        hsl = slice(h * HEAD_DIM, (h + 1) * HEAD_DIM)
        o = o_ref[0, :, hsl] + o_ref[1, :, hsl]
        y = o * lax.rsqrt(jnp.mean(o * o, axis=-1, keepdims=True) + EPS) * og_ref[...]
        y_ref[0, :, hsl] = (y * _silu(zx_ref[0, :, hsl].astype(F32))).astype(BF16)


def _gdn(qx, kx, vx, zx, gx, qh, kh, vh, gh, o_g):
    b, n, cdim = qx.shape
    nh = qh.shape[1]
    hg = GDN_HG
    ng = C_HEADS // hg
    ncol = 4 * hg
    assert n % (CHUNK * GDN_PREP_UNROLL) == 0 and nh % CHUNK == 0 and CHUNK == 64

    def regroup(gates, rows):
        t = gates.reshape(b, rows, 2, 2, ng, hg)
        cols = t.transpose(0, 4, 1, 2, 3, 5).reshape(b, ng, rows, ncol)
        gc = t[:, :, :, 0].reshape(b, rows // CHUNK, CHUNK, 2, ng, hg)
        return cols, gc.transpose(0, 4, 1, 3, 5, 2).reshape(b, ng, rows // CHUNK, GDN_PACK)

    gxc, gxr = regroup(gx, n)
    ghc, ghr = regroup(gh, nh)
    wdt = hg * HEAD_DIM
    tot = n + nh
    seq = lambda rows: pl.BlockSpec((1, rows, wdt), lambda i, j: (i, 0, j))
    gcol = lambda rows: pl.BlockSpec((1, 1, rows, ncol), lambda i, j: (i, j, 0, 0))
    grow = lambda rows: pl.BlockSpec((1, 1, rows // CHUNK, GDN_PACK), lambda i, j: (i, j, 0, 0))
    return pl.pallas_call(
        _gdn_kernel,
        out_shape=jax.ShapeDtypeStruct((b, n, cdim), BF16),
        grid=(b, ng),
        in_specs=[seq(n), seq(n), seq(n), seq(n), gcol(n), grow(n),
                  seq(nh), seq(nh), seq(nh), gcol(nh), grow(nh), _resident((1, HEAD_DIM))],
        out_specs=seq(n),
        scratch_shapes=[
            pltpu.VMEM((GDN_CHAINS, tot, HEAD_DIM), F32),
            pltpu.VMEM((GDN_CHAINS, tot, HEAD_DIM), BF16),
            pltpu.VMEM((GDN_CHAINS, tot, HEAD_DIM), BF16),
            pltpu.VMEM((GDN_CHAINS, tot, HEAD_DIM), BF16),
            pltpu.VMEM((GDN_CHAINS, tot, CHUNK), BF16),
            pltpu.VMEM((2, n, wdt), F32),
            pltpu.VMEM((GDN_CHAINS, HEAD_DIM, HEAD_DIM), F32),
        ],
        compiler_params=_params(2),
        name="gated_delta",
    )(qx, kx, vx, zx, gxc, gxr, qh, kh, vh, ghc, ghr, o_g)


def _gate_columns(vals):
    zeros = jnp.zeros_like(vals[0])
    return jnp.concatenate([vals[0], zeros, vals[1], zeros])[None].astype(F32)


def kernel(x, c, ctx, c_ctx, w_mod, b_mod, norm_g, ffn_wg, ffn_wu, ffn_wd, ab_w_in, ab_q_norm, ab_k_norm,
           pool_w, pool_scale, ab_w_out, gdn_w_in, gdn_conv_w, gdn_a_log, gdn_dt_bias, gdn_o_norm, gdn_w_out):
    depth = w_mod.shape[0]
    b, n, d = x.shape
    n_ctx = ctx.shape[1]
    tm_x = min(512, n)
    tm_h = n_ctx

    pad = (-(b + 1)) % SUBLANES
    cond = jnp.concatenate([c, c_ctx[None], jnp.zeros((pad, d), F32)], axis=0)
    mods = _modulation(cond, w_mod, b_mod)
    cos_t, sin_t = _rope_tables(n)

    h = ctx
    for i in range(depth):
        last = i == depth - 1
        mx = mods[i, :b].reshape(b, N_MOD, d)
        mh = mods[i, b:b + 1].reshape(1, N_MOD, d)
        wg, wu, wd = (t[i].astype(BF16) for t in (ffn_wg, ffn_wu, ffn_wd))
        g0, g1, g2 = (norm_g[i, s][None] for s in range(3))
        x = _ffn(x, mx, g0, wg[0], wu[0], wd[0], s=0, tm=tm_x)
        h = _ffn(h, mh, g0, wg[0], wu[0], wd[0], s=0, tm=tm_h)
        j = i // 2
        if i % 2 == 0:
            w_in = ab_w_in[j].astype(BF16)
            pw = pool_w[j].astype(BF16)
            common = (w_in, ab_q_norm[j][None], ab_k_norm[j][None], pw, pool_scale[j][None])
            qx, kx, vx, px = _ab_proj(x, mx, g1, *common, cos_t, sin_t, tm=tm_x)
            qh, kh, vh, ph = _ab_proj(h, mh, g1, *common, None, None, tm=tm_h)
            ax = _attention(qx, [(kh, vh), (kx, vx)], tq=min(256, n), tb=min(1024, n))
            mix_x = (ax, px)
            w_out = ab_w_out[j].astype(BF16)
            if not last:
                ah = _attention(qh, [(kh, vh)], tq=min(256, n_ctx), tb=min(256, n_ctx))
                mix_h = (ah, ph)
        else:
            w_in = gdn_w_in[j].astype(BF16)
            al = _gate_columns(gdn_a_log[j])
            dt = _gate_columns(gdn_dt_bias[j])
            qx, kx, vx, zx, gx = _gdn_proj(x, mx, g1, w_in, gdn_conv_w[j], al, dt, tm=tm_x)
            qh, kh, vh, zh, gh = _gdn_proj(h, mh, g1, w_in, gdn_conv_w[j], al, dt, tm=tm_h)
            if not last:
                raise NotImplementedError("context output of the DeltaNet mixer is only needed for depth > 2")
            mix_x = (_gdn(qx, kx, vx, zx, gx, qh, kh, vh, gh, gdn_o_norm[j][None]),)
            w_out = gdn_w_out[j].astype(BF16)
        x = _ffn(x, mx, g2, wg[1], wu[1], wd[1], s=2, tm=tm_x, mix=mix_x, w_out=w_out)
        if not last:
            h = _ffn(h, mh, g2, wg[1], wu[1], wd[1], s=2, tm=tm_h, mix=mix_h, w_out=w_out)
    return x
```
